```python
import math
import jax, jax.numpy as jnp
from jax import lax
import numpy as np

D_MODEL = 1024
BATCH = 8
SEQ = 4096
DEPTH = 4

N_EVEN = (DEPTH + 1) // 2
N_ODD = DEPTH // 2
QBLK = 128
EPS = 1e-6
MASK_VALUE = -1e30

A_WIDTH = D_MODEL // 2
A_HEAD_DIM = 64
A_HEADS = A_WIDTH // (2 * A_HEAD_DIM)
B_WIDTH = D_MODEL - A_WIDTH
POOL_WINDOWS = (2, 4, 8, 16)
B_GROUPS = len(POOL_WINDOWS)
B_GROUP_DIM = B_WIDTH // B_GROUPS
AB_IN = 3 * A_WIDTH + B_WIDTH
C_HEAD_DIM = 64
C_Q_HEADS = D_MODEL // C_HEAD_DIM
C_KV_HEADS = 2
C_GROUP = C_Q_HEADS // C_KV_HEADS
C_WINDOW = 128
C_IN = (C_Q_HEADS + 2 * C_KV_HEADS) * C_HEAD_DIM
D_FF = 2816
CONV_W = 3

kernel_name = "hybrid_diffattn_pool_swa_sink_convffn"


def rms_norm(x, g):
    xf = x.astype(jnp.float32)
    y = xf * lax.rsqrt(jnp.mean(xf * xf, axis=-1, keepdims=True) + EPS)
    return (y * g.astype(jnp.float32)).astype(x.dtype)


def alibi_slopes(n):
    return 2.0 ** (-8.0 * jnp.arange(1, n + 1, dtype=jnp.float32) / n)


def diff_attention(q, k, v, lam, lam_init, sub_g):
    B, S = q.shape[0], q.shape[1]
    nb = S // QBLK
    scale = A_HEAD_DIM ** -0.5
    slopes = alibi_slopes(A_HEADS)
    kf = k.astype(jnp.float32)
    vf = v.astype(jnp.float32)
    key_pos = jnp.arange(S)
    qb = q.reshape(B, nb, QBLK, A_HEADS, 2, A_HEAD_DIM).transpose(1, 0, 2, 3, 4, 5)

    def block(args):
        qi, i = args
        qpos = i * QBLK + jnp.arange(QBLK)
        rel = (qpos[:, None] - key_pos[None, :]).astype(jnp.float32)
        bias = jnp.where(rel[None] >= 0, -slopes[:, None, None] * rel[None], MASK_VALUE)
        s = jnp.einsum('bqhmd,bkhmd->bhmqk', qi.astype(jnp.float32), kf) * scale + bias[None, :, None]
        p = jax.nn.softmax(s, axis=-1)
        attn = p[:, :, 0] - lam * p[:, :, 1]
        return jnp.einsum('bhqk,bkhe->bqhe', attn, vf)

    o = lax.map(block, (qb, jnp.arange(nb)))
    o = o.transpose(1, 0, 2, 3, 4).reshape(B, S, A_HEADS, 2 * A_HEAD_DIM)
    o = rms_norm(o, sub_g) * (1.0 - lam_init)
    return o.reshape(B, S, A_WIDTH)


def pool_mixer(u, w_group, scale):
    B, S = u.shape[0], u.shape[1]
    uf = u.astype(jnp.float32)
    cs = jnp.pad(lax.cumsum(uf, axis=1), ((0, 0), (1, 0), (0, 0)))
    pos = jnp.arange(S)
    outs = []
    for g, w in enumerate(POOL_WINDOWS):
        lo_c, hi_c = g * B_GROUP_DIM, (g + 1) * B_GROUP_DIM
        csg = cs[:, :, lo_c:hi_c]
        start = jnp.maximum(pos + 1 - w, 0)
        win_sum = csg[:, 1:] - csg[:, start]
        cnt = jnp.minimum(pos + 1, w).astype(jnp.float32)
        outs.append(win_sum / cnt[None, :, None] - uf[:, :, lo_c:hi_c])
    pooled = jnp.stack(outs, axis=2)
    mixed = jnp.einsum('bsgc,gcd->bsgd', pooled, w_group.astype(jnp.float32))
    return mixed.reshape(B, S, B_WIDTH) * scale.astype(jnp.float32)


def sliding_window_attention(q, k, v, sinks):
    B, S = q.shape[0], q.shape[1]
    nb = S // QBLK
    scale = C_HEAD_DIM ** -0.5
    slopes = alibi_slopes(C_Q_HEADS).reshape(C_KV_HEADS, C_GROUP)
    sink = sinks.astype(jnp.float32).reshape(C_KV_HEADS, C_GROUP)[None, :, :, None, None]
    pad = ((0, 0), (QBLK, 0), (0, 0), (0, 0))
    kp = jnp.pad(k.astype(jnp.float32), pad)
    vp = jnp.pad(v.astype(jnp.float32), pad)
    qb = q.reshape(B, nb, QBLK, C_KV_HEADS, C_GROUP, C_HEAD_DIM).transpose(1, 0, 2, 3, 4, 5)
    a_idx = jnp.arange(QBLK)[:, None]
    j_idx = jnp.arange(2 * QBLK)[None, :]
    rel = QBLK + a_idx - j_idx
    alibi = -slopes[:, :, None, None] * rel.astype(jnp.float32)

    def block(args):
        qi, i = args
        kb = lax.dynamic_slice_in_dim(kp, i * QBLK, 2 * QBLK, axis=1)
        vb = lax.dynamic_slice_in_dim(vp, i * QBLK, 2 * QBLK, axis=1)
        key_pos = (i - 1) * QBLK + j_idx
        valid = (rel >= 0) & (rel < C_WINDOW) & (key_pos >= 0)
        bias = jnp.where(valid, alibi, MASK_VALUE)
        s = jnp.einsum('bqhgd,bkhd->bhgqk', qi.astype(jnp.float32), kb) * scale + bias
        m = jnp.maximum(jnp.max(s, axis=-1, keepdims=True), sink)
        p = jnp.exp(s - m)
        denom = jnp.sum(p, axis=-1, keepdims=True) + jnp.exp(sink - m)
        return jnp.einsum('bhgqk,bkhd->bqhgd', p / denom, vb)

    o = lax.map(block, (qb, jnp.arange(nb)))
    return o.transpose(1, 0, 2, 3, 4, 5).reshape(B, S, C_Q_HEADS * C_HEAD_DIM)


def diff_pool_layer(x, layer_idx, norm_g, w_in, q_g, k_g, lam_p, sub_g, w_group, pool_scale, w_out):
    B, S = x.shape[0], x.shape[1]
    proj = rms_norm(x, norm_g) @ w_in
    q, k, v, u = jnp.split(proj, [A_WIDTH, 2 * A_WIDTH, 3 * A_WIDTH], axis=-1)
    q = rms_norm(q.reshape(B, S, A_HEADS, 2, A_HEAD_DIM), q_g)
    k = rms_norm(k.reshape(B, S, A_HEADS, 2, A_HEAD_DIM), k_g)
    v = v.reshape(B, S, A_HEADS, 2 * A_HEAD_DIM)
    lam_init = 0.8 - 0.6 * math.exp(-0.3 * layer_idx)
    lp = lam_p.astype(jnp.float32)
    lam = jnp.exp(jnp.sum(lp[0] * lp[1])) - jnp.exp(jnp.sum(lp[2] * lp[3])) + lam_init
    a_out = diff_attention(q, k, v, lam, lam_init, sub_g)
    b_out = pool_mixer(u, w_group, pool_scale)
    mix = jnp.concatenate([a_out.astype(jnp.float32), b_out], axis=-1).astype(x.dtype)
    return x + mix @ w_out


def swa_layer(x, norm_g, w_in, q_g, k_g, sinks, w_out):
    B, S = x.shape[0], x.shape[1]
    proj = rms_norm(x, norm_g) @ w_in
    kv_w = C_KV_HEADS * C_HEAD_DIM
    q, k, v = jnp.split(proj, [C_Q_HEADS * C_HEAD_DIM, C_Q_HEADS * C_HEAD_DIM + kv_w], axis=-1)
    q = rms_norm(q.reshape(B, S, C_KV_HEADS, C_GROUP, C_HEAD_DIM), q_g)
    k = rms_norm(k.reshape(B, S, C_KV_HEADS, C_HEAD_DIM), k_g)
    v = v.reshape(B, S, C_KV_HEADS, C_HEAD_DIM)
    o = sliding_window_attention(q, k, v, sinks).astype(x.dtype)
    return x + o @ w_out


def conv_glu_ffn(x, norm_g, w_up, conv_w, conv_b, w_down):
    S = x.shape[1]
    h = rms_norm(x, norm_g) @ w_up
    hp = jnp.pad(h, ((0, 0), (CONV_W - 1, 0), (0, 0)))
    c = conv_b
    for j in range(CONV_W):
        c = c + conv_w[j] * hp[:, j:j + S]
    gate, up = jnp.split(c, 2, axis=-1)
    return x + (jax.nn.silu(gate) * up) @ w_down


def setup_inputs(seed: int = 0) -> dict:
    key = jax.random.key(seed)
    ks = jax.random.split(key, 24)
    f32 = jnp.float32
    nrm = lambda k, shape, s: jax.random.normal(k, shape, f32) * s
    return {
        "x": nrm(ks[0], (BATCH, SEQ, D_MODEL), 1.0),
        "ab_norm": 1.0 + nrm(ks[1], (N_EVEN, D_MODEL), 0.02),
        "ab_w_in": nrm(ks[2], (N_EVEN, D_MODEL, AB_IN), D_MODEL ** -0.5),
        "a_q_norm": 1.0 + nrm(ks[3], (N_EVEN, A_HEAD_DIM), 0.02),
        "a_k_norm": 1.0 + nrm(ks[4], (N_EVEN, A_HEAD_DIM), 0.02),
        "a_lambda": nrm(ks[5], (N_EVEN, 4, A_HEAD_DIM), 0.1),
        "a_sub_norm": 1.0 + nrm(ks[6], (N_EVEN, 2 * A_HEAD_DIM), 0.02),
        "b_w_group": nrm(ks[7], (N_EVEN, B_GROUPS, B_GROUP_DIM, B_GROUP_DIM), B_GROUP_DIM ** -0.5),
        "b_scale": 1.0 + nrm(ks[8], (N_EVEN, B_WIDTH), 0.1),
        "ab_w_out": nrm(ks[9], (N_EVEN, A_WIDTH + B_WIDTH, D_MODEL), (A_WIDTH + B_WIDTH) ** -0.5),
        "c_norm": 1.0 + nrm(ks[10], (N_ODD, D_MODEL), 0.02),
        "c_w_in": nrm(ks[11], (N_ODD, D_MODEL, C_IN), D_MODEL ** -0.5),
        "c_q_norm": 1.0 + nrm(ks[12], (N_ODD, C_HEAD_DIM), 0.02),
        "c_k_norm": 1.0 + nrm(ks[13], (N_ODD, C_HEAD_DIM), 0.02),
        "c_sinks": nrm(ks[14], (N_ODD, C_Q_HEADS), 0.5),
        "c_w_out": nrm(ks[15], (N_ODD, C_Q_HEADS * C_HEAD_DIM, D_MODEL), (C_Q_HEADS * C_HEAD_DIM) ** -0.5),
        "f_norm": 1.0 + nrm(ks[16], (DEPTH, D_MODEL), 0.02),
        "f_w_up": nrm(ks[17], (DEPTH, D_MODEL, 2 * D_FF), D_MODEL ** -0.5),
        "f_conv": nrm(ks[18], (DEPTH, CONV_W, 2 * D_FF), CONV_W ** -0.5),
        "f_conv_b": nrm(ks[19], (DEPTH, 2 * D_FF), 0.02),
        "f_w_down": nrm(ks[20], (DEPTH, D_FF, D_MODEL), D_FF ** -0.5),
    }


def reference(x, ab_norm, ab_w_in, a_q_norm, a_k_norm, a_lambda, a_sub_norm, b_w_group, b_scale,
              ab_w_out, c_norm, c_w_in, c_q_norm, c_k_norm, c_sinks, c_w_out,
              f_norm, f_w_up, f_conv, f_conv_b, f_w_down):
    for layer in range(DEPTH):
        if layer % 2 == 0:
            e = layer // 2
            x = diff_pool_layer(x, layer, ab_norm[e], ab_w_in[e], a_q_norm[e], a_k_norm[e],
                                a_lambda[e], a_sub_norm[e], b_w_group[e], b_scale[e], ab_w_out[e])
        else:
            o = layer // 2
            x = swa_layer(x, c_norm[o], c_w_in[o], c_q_norm[o], c_k_norm[o], c_sinks[o], c_w_out[o])
        x = conv_glu_ffn(x, f_norm[layer], f_w_up[layer], f_conv[layer], f_conv_b[layer], f_w_down[layer])
    return x
```

```python
import functools
import math

import jax
import jax.numpy as jnp
from jax import lax
from jax.experimental import pallas as pl
from jax.experimental.pallas import tpu as pltpu

D_MODEL = 1024
BATCH = 8
SEQ = 4096
DEPTH = 4
EPS = 1e-6
MASK_VALUE = -1e30

A_WIDTH = 512
A_HEAD_DIM = 64
A_HEADS = 4
A_VDIM = 2 * A_HEAD_DIM
B_WIDTH = 512
POOL_WINDOWS = (2, 4, 8, 16)
B_GROUP_DIM = 128
POOL_HALO = 16

C_HEAD_DIM = 64
C_Q_HEADS = 16
C_KV_HEADS = 2
C_GROUP = 8
C_WINDOW = 128
C_QBLK = 128
C_QW = C_Q_HEADS * C_HEAD_DIM
C_PAIRS = C_GROUP // 2

D_FF = 2816
CONV_W = 3
FF_CHUNK = 256
FF_NCHUNK = D_FF // FF_CHUNK
CONV_HALO = 8

LANES = 128
NORM_SEG = 256

ROW_TILE = 512
ATT_TILE = 256
VMEM_LIMIT = 56 * 1024 * 1024

N_ROWS = BATCH * SEQ
F32 = jnp.float32
BF16 = jnp.bfloat16


def _params(sem, vmem=VMEM_LIMIT):
    return pltpu.CompilerParams(dimension_semantics=sem, vmem_limit_bytes=vmem)


def _const_spec(shape):
    nd = len(shape)
    return pl.BlockSpec(shape, lambda *_: (0,) * nd, pipeline_mode=pl.Buffered(1))


def _rms_rows(x, g):
    ms = jnp.mean(x * x, axis=-1, keepdims=True)
    return x * lax.rsqrt(ms + EPS) * g


def _seg_norm(seg, bd, gain):
    ss = jnp.dot((seg * seg).astype(BF16), bd, preferred_element_type=F32)
    return seg * lax.rsqrt(ss * (1.0 / 64.0) + EPS) * gain


def _ab_in_kernel(x_ref, g_ref, w_ref, bd_ref, qg_ref, kg_ref, q_ref, k_ref, v_ref, u_ref):
    xn = _rms_rows(x_ref[...], g_ref[...]).astype(BF16)
    bd = bd_ref[...]
    for cb in range(A_WIDTH // NORM_SEG):
        lo, hi = cb * NORM_SEG, (cb + 1) * NORM_SEG
        qseg = jnp.dot(xn, w_ref[:, lo:hi], preferred_element_type=F32)
        q_ref[:, lo:hi] = _seg_norm(qseg, bd, qg_ref[:, lo:hi]).astype(BF16)
        kseg = jnp.dot(xn, w_ref[:, A_WIDTH + lo:A_WIDTH + hi], preferred_element_type=F32)
        k_ref[:, lo:hi] = _seg_norm(kseg, bd, kg_ref[:, lo:hi]).astype(BF16)
    v = jnp.dot(xn, w_ref[:, 2 * A_WIDTH:3 * A_WIDTH], preferred_element_type=F32).astype(BF16)
    ones = jnp.ones((v.shape[0], A_VDIM), BF16)
    for h in range(A_HEADS):
        v_ref[:, h * 2 * A_VDIM:h * 2 * A_VDIM + A_VDIM] = v[:, h * A_VDIM:(h + 1) * A_VDIM]
        v_ref[:, h * 2 * A_VDIM + A_VDIM:(h + 1) * 2 * A_VDIM] = ones
    u_ref[...] = jnp.dot(xn, w_ref[:, 3 * A_WIDTH:], preferred_element_type=F32)


def _ab_in(x, g, w, bd, qg, kg):
    tm = ROW_TILE
    row = lambda i: (i, 0)
    return pl.pallas_call(
        _ab_in_kernel,
        grid=(N_ROWS // tm,),
        in_specs=[pl.BlockSpec((tm, D_MODEL), row), _const_spec((1, D_MODEL)),
                  _const_spec((D_MODEL, 3 * A_WIDTH + B_WIDTH)), _const_spec((NORM_SEG, NORM_SEG)),
                  _const_spec((1, A_WIDTH)), _const_spec((1, A_WIDTH))],
        out_specs=[pl.BlockSpec((tm, A_WIDTH), row), pl.BlockSpec((tm, A_WIDTH), row),
                   pl.BlockSpec((tm, 2 * A_WIDTH), row), pl.BlockSpec((tm, B_WIDTH), row)],
        out_shape=[jax.ShapeDtypeStruct((N_ROWS, A_WIDTH), BF16), jax.ShapeDtypeStruct((N_ROWS, A_WIDTH), BF16),
                   jax.ShapeDtypeStruct((N_ROWS, 2 * A_WIDTH), BF16), jax.ShapeDtypeStruct((N_ROWS, B_WIDTH), F32)],
        compiler_params=_params(("arbitrary",)),
        name="ab_in",
    )(x, g, w, bd, qg, kg)


def _diff_attn_kernel(slope_ref, lam_ref, q_ref, k_ref, v_ref, d_ref, sg_ref, o_ref, qs_ref, m_ref, acc_ref):
    T = ATT_TILE
    h = pl.program_id(1)
    i = pl.program_id(2)
    q = q_ref[...]
    lane = lax.broadcasted_iota(jnp.int32, q.shape, 1)
    zero = jnp.zeros_like(q)
    qs_ref[0:T, :] = jnp.where(lane < A_HEAD_DIM, q, zero)
    qs_ref[T:2 * T, :] = jnp.where(lane >= A_HEAD_DIM, q, zero)
    m_ref[...] = jnp.full(m_ref.shape, MASK_VALUE, F32)
    acc_ref[...] = jnp.zeros(acc_ref.shape, F32)
    slope = slope_ref[h]
    dtab = d_ref[...]

    def step(j, masked):
        koff = pl.multiple_of(j * T, T)
        kt = k_ref[pl.ds(koff, T), :]
        vt = v_ref[pl.ds(koff, T), :]
        s = lax.dot_general(qs_ref[...], kt, (((1,), (1,)), ((), ())), preferred_element_type=F32)
        bias = dtab
        if masked:
            r = lax.broadcasted_iota(jnp.int32, (T, T), 0)
            c = lax.broadcasted_iota(jnp.int32, (T, T), 1)
            bias = jnp.where(c > r, MASK_VALUE, dtab)
        s = s + jnp.concatenate([bias, bias], axis=0)
        off = slope * ((j - i) * T).astype(F32)
        m_prev = m_ref[...]
        m_new = jnp.maximum(m_prev, jnp.max(s, axis=1, keepdims=True) + off)
        alpha = jnp.exp(m_prev - m_new)
        shift = m_new - off
        p = jnp.exp(s - jnp.concatenate([shift] * (T // LANES), axis=1))
        pv = jnp.dot(p.astype(BF16), vt, preferred_element_type=F32)
        acc_ref[...] = acc_ref[...] * jnp.concatenate([alpha, alpha], axis=1) + pv
        m_ref[...] = m_new

    def body(j, carry):
        step(j, False)
        return carry

    lax.fori_loop(0, i, body, 0)
    step(i, True)

    acc = acc_ref[...]
    o0 = acc[0:T, 0:A_VDIM] / acc[0:T, A_VDIM:]
    o1 = acc[T:, 0:A_VDIM] / acc[T:, A_VDIM:]
    o = o0 - lam_ref[0] * o1
    o_ref[...] = _rms_rows(o, sg_ref[...]).astype(BF16)


def _diff_attn(slopes, lam, q, k, vaug, dtab, subg):
    T = ATT_TILE
    nq = SEQ // T
    smem = pl.BlockSpec(memory_space=pltpu.SMEM)
    return pl.pallas_call(
        _diff_attn_kernel,
        grid=(BATCH, A_HEADS, nq),
        in_specs=[smem, smem,
                  pl.BlockSpec((T, A_VDIM), lambda b, h, i: (b * nq + i, h)),
                  pl.BlockSpec((SEQ, A_VDIM), lambda b, h, i: (b, h)),
                  pl.BlockSpec((SEQ, 2 * A_VDIM), lambda b, h, i: (b, h)),
                  pl.BlockSpec((None, T, T), lambda b, h, i: (h, 0, 0)),
                  pl.BlockSpec((1, A_VDIM), lambda b, h, i: (0, 0))],
        out_specs=pl.BlockSpec((T, A_VDIM), lambda b, h, i: (b * nq + i, h)),
        out_shape=jax.ShapeDtypeStruct((N_ROWS, A_WIDTH), BF16),
        scratch_shapes=[pltpu.VMEM((2 * T, A_VDIM), BF16), pltpu.VMEM((2 * T, LANES), F32),
                        pltpu.VMEM((2 * T, 2 * A_VDIM), F32)],
        compiler_params=_params(("arbitrary", "arbitrary", "arbitrary")),
        name="diff_attn",
    )(slopes, lam, q, k, vaug, dtab, subg)


def _ab_out_kernel(a_ref, u_ref, x_ref, wg_ref, ps_ref, wo_ref, o_ref, uext_ref, carry_ref):
    tm = ROW_TILE
    tiles_per_seq = SEQ // tm
    it = pl.program_id(0) % tiles_per_seq

    @pl.when(it == 0)
    def _():
        carry_ref[...] = jnp.zeros(carry_ref.shape, F32)

    u = u_ref[...]
    uext_ref[0:POOL_HALO, :] = carry_ref[...]
    uext_ref[POOL_HALO:, :] = u
    carry_ref[...] = u[tm - POOL_HALO:, :]
    pos = it * tm + lax.broadcasted_iota(jnp.int32, (tm, 1), 0)
    mixed = []
    for g, w in enumerate(POOL_WINDOWS):
        lo, hi = g * B_GROUP_DIM, (g + 1) * B_GROUP_DIM
        ug = u[:, lo:hi]
        win = ug
        for d in range(1, w):
            win = win + uext_ref[pl.ds(POOL_HALO - d, tm), lo:hi]
        cnt = jnp.minimum(pos + 1, w).astype(F32)
        pooled = win / cnt - ug
        mg = jnp.dot(pooled.astype(BF16), wg_ref[g], preferred_element_type=F32)
        mixed.append((mg * ps_ref[:, lo:hi]).astype(BF16))
    mix = jnp.concatenate([a_ref[...]] + mixed, axis=1)
    o_ref[...] = x_ref[...] + jnp.dot(mix, wo_ref[...], preferred_element_type=F32)


def _ab_out(a, u, x, wg, ps, wo):
    tm = ROW_TILE
    row = lambda i: (i, 0)
    return pl.pallas_call(
        _ab_out_kernel,
        grid=(N_ROWS // tm,),
        in_specs=[pl.BlockSpec((tm, A_WIDTH), row), pl.BlockSpec((tm, B_WIDTH), row),
                  pl.BlockSpec((tm, D_MODEL), row), _const_spec((len(POOL_WINDOWS), B_GROUP_DIM, B_GROUP_DIM)),
                  _const_spec((1, B_WIDTH)), _const_spec((A_WIDTH + B_WIDTH, D_MODEL))],
        out_specs=pl.BlockSpec((tm, D_MODEL), row),
        out_shape=jax.ShapeDtypeStruct((N_ROWS, D_MODEL), F32),
        scratch_shapes=[pltpu.VMEM((tm + POOL_HALO, B_WIDTH), F32), pltpu.VMEM((POOL_HALO, B_WIDTH), F32)],
        compiler_params=_params(("arbitrary",)),
        name="ab_out",
    )(a, u, x, wg, ps, wo)


def _c_in_kernel(x_ref, g_ref, w_ref, bd_ref, qg_ref, kg_ref, q_ref, k_ref, v_ref):
    xn = _rms_rows(x_ref[...], g_ref[...]).astype(BF16)
    bd = bd_ref[...]
    for cb in range(C_QW // NORM_SEG):
        lo, hi = cb * NORM_SEG, (cb + 1) * NORM_SEG
        qseg = jnp.dot(xn, w_ref[:, lo:hi], preferred_element_type=F32)
        q_ref[:, lo:hi] = _seg_norm(qseg, bd, qg_ref[:, lo:hi]).astype(BF16)
    kseg = jnp.dot(xn, w_ref[:, C_QW:C_QW + NORM_SEG], preferred_element_type=F32)
    k_ref[...] = _seg_norm(kseg, bd, kg_ref[...]).astype(BF16)
    v_ref[...] = jnp.dot(xn, w_ref[:, C_QW + NORM_SEG:], preferred_element_type=F32).astype(BF16)


def _c_in(x, g, w, bd, qg, kg):
    tm = ROW_TILE
    row = lambda i: (i, 0)
    kvw = 2 * C_KV_HEADS * C_HEAD_DIM
    return pl.pallas_call(
        _c_in_kernel,
        grid=(N_ROWS // tm,),
        in_specs=[pl.BlockSpec((tm, D_MODEL), row), _const_spec((1, D_MODEL)),
                  _const_spec((D_MODEL, C_QW + 2 * kvw)), _const_spec((NORM_SEG, NORM_SEG)),
                  _const_spec((1, C_QW)), _const_spec((1, kvw))],
        out_specs=[pl.BlockSpec((tm, C_QW), row), pl.BlockSpec((tm, kvw), row), pl.BlockSpec((tm, kvw), row)],
        out_shape=[jax.ShapeDtypeStruct((N_ROWS, C_QW), BF16), jax.ShapeDtypeStruct((N_ROWS, kvw), BF16),
                   jax.ShapeDtypeStruct((N_ROWS, kvw), BF16)],
        compiler_params=_params(("arbitrary",)),
        name="c_in",
    )(x, g, w, bd, qg, kg)


def _swa_kernel(q_ref, kp_ref, kc_ref, vp_ref, vc_ref, bias_ref, sink_ref, o_ref):
    Q = C_QBLK
    i = pl.program_id(1)
    q = q_ref[...]
    lane = lax.broadcasted_iota(jnp.int32, (Q, LANES), 1)
    first_half = lane < C_HEAD_DIM
    zero = jnp.zeros((Q, LANES), BF16)
    rows = []
    for p in range(C_PAIRS):
        qp = q[:, p * LANES:(p + 1) * LANES]
        rows.append(jnp.where(first_half, qp, zero))
        rows.append(jnp.where(first_half, zero, qp))
    qs = jnp.concatenate(rows, axis=0)
    k2 = jnp.concatenate([kp_ref[...], kc_ref[...]], axis=0)
    v2 = jnp.concatenate([vp_ref[...], vc_ref[...]], axis=0)
    s = lax.dot_general(qs, k2, (((1,), (1,)), ((), ())), preferred_element_type=F32)
    s = s.reshape(C_GROUP, Q, 2 * Q) + bias_ref[...]
    col = lax.broadcasted_iota(jnp.int32, (C_GROUP, Q, 2 * Q), 2)
    s = jnp.where(jnp.logical_and(i == 0, col < Q), MASK_VALUE, s)
    sink = sink_ref[...][:, :, 0:1]
    m = jnp.maximum(jnp.max(s, axis=-1, keepdims=True), sink)
    p = jnp.exp(s - m)
    denom = jnp.sum(p, axis=-1, keepdims=True) + jnp.exp(sink - m)
    pn = (p / denom).reshape(C_GROUP * Q, 2 * Q).astype(BF16)
    o = jnp.dot(pn, v2, preferred_element_type=F32)
    for p_ in range(C_PAIRS):
        oa = o[(2 * p_) * Q:(2 * p_ + 1) * Q, :]
        ob = o[(2 * p_ + 1) * Q:(2 * p_ + 2) * Q, :]
        o_ref[:, p_ * LANES:(p_ + 1) * LANES] = jnp.where(first_half, oa, ob).astype(BF16)


def _swa(q, kdup, vdup, bias, sink):
    Q = C_QBLK
    nb = SEQ // Q
    gw = C_GROUP * C_HEAD_DIM
    cur = lambda b, i, g: (b * nb + i, g)
    prev = lambda b, i, g: (b * nb + jnp.maximum(i - 1, 0), g)
    return pl.pallas_call(
        _swa_kernel,
        grid=(BATCH, nb, C_KV_HEADS),
        in_specs=[pl.BlockSpec((Q, gw), cur),
                  pl.BlockSpec((Q, LANES), prev), pl.BlockSpec((Q, LANES), cur),
                  pl.BlockSpec((Q, LANES), prev), pl.BlockSpec((Q, LANES), cur),
                  pl.BlockSpec((C_GROUP, Q, 2 * Q), lambda b, i, g: (g, 0, 0)),
                  pl.BlockSpec((C_GROUP, 1, LANES), lambda b, i, g: (g, 0, 0))],
        out_specs=pl.BlockSpec((Q, gw), cur),
        out_shape=jax.ShapeDtypeStruct((N_ROWS, C_QW), BF16),
        compiler_params=_params(("arbitrary", "arbitrary", "arbitrary")),
        name="swa",
    )(q, kdup, kdup, vdup, vdup, bias, sink)


def _c_out_kernel(o_ref, x_ref, w_ref, y_ref):
    y_ref[...] = x_ref[...] + jnp.dot(o_ref[...], w_ref[...], preferred_element_type=F32)


def _c_out(o, x, w):
    tm = ROW_TILE
    row = lambda i: (i, 0)
    return pl.pallas_call(
        _c_out_kernel,
        grid=(N_ROWS // tm,),
        in_specs=[pl.BlockSpec((tm, C_QW), row), pl.BlockSpec((tm, D_MODEL), row), _const_spec((C_QW, D_MODEL))],
        out_specs=pl.BlockSpec((tm, D_MODEL), row),
        out_shape=jax.ShapeDtypeStruct((N_ROWS, D_MODEL), F32),
        compiler_params=_params(("arbitrary",)),
        name="c_out",
    )(o, x, w)


def _ffn_kernel(x_ref, g_ref, wg_ref, wu_ref, cwg_ref, cwu_ref, cbg_ref, cbu_ref, wd_ref, o_ref,
                hs_ref, carry_ref, act_ref):
    tm = ROW_TILE
    H = CONV_HALO

    @pl.when(pl.program_id(0) % (SEQ // tm) == 0)
    def _():
        carry_ref[...] = jnp.zeros(carry_ref.shape, F32)

    x = x_ref[...]
    xn = _rms_rows(x, g_ref[...]).astype(BF16)

    def conv(c, part, w_ref, cw_ref, cb_ref):
        h = jnp.dot(xn, w_ref[c], preferred_element_type=F32)
        buf = hs_ref.at[c % 2, part]
        buf[0:H, :] = carry_ref[part, c]
        buf[H:, :] = h
        carry_ref[part, c] = h[tm - H:, :]
        cw = cw_ref[c]
        return (cb_ref[c] + cw[2:3, :] * h + cw[1:2, :] * buf[pl.ds(H - 1, tm), :]
                + cw[0:1, :] * buf[pl.ds(H - 2, tm), :])

    for c in range(FF_NCHUNK):
        gate = conv(c, 0, wg_ref, cwg_ref, cbg_ref)
        up = conv(c, 1, wu_ref, cwu_ref, cbu_ref)
        act = gate / (1.0 + jnp.exp(-gate)) * up
        act_ref[:, c * FF_CHUNK:(c + 1) * FF_CHUNK] = act.astype(BF16)
    o_ref[...] = x + jnp.dot(act_ref[...], wd_ref[...], preferred_element_type=F32)


def _ffn(x, g, wg, wu, cwg, cwu, cbg, cbu, wd):
    tm = ROW_TILE
    row = lambda i: (i, 0)
    return pl.pallas_call(
        _ffn_kernel,
        grid=(N_ROWS // tm,),
        in_specs=[pl.BlockSpec((tm, D_MODEL), row), _const_spec((1, D_MODEL)),
                  _const_spec((FF_NCHUNK, D_MODEL, FF_CHUNK)), _const_spec((FF_NCHUNK, D_MODEL, FF_CHUNK)),
                  _const_spec((FF_NCHUNK, CONV_W, FF_CHUNK)), _const_spec((FF_NCHUNK, CONV_W, FF_CHUNK)),
                  _const_spec((FF_NCHUNK, 1, FF_CHUNK)), _const_spec((FF_NCHUNK, 1, FF_CHUNK)),
                  _const_spec((D_FF, D_MODEL))],
        out_specs=pl.BlockSpec((tm, D_MODEL), row),
        out_shape=jax.ShapeDtypeStruct((N_ROWS, D_MODEL), F32),
        scratch_shapes=[pltpu.VMEM((2, 2, tm + CONV_HALO, FF_CHUNK), F32),
                        pltpu.VMEM((2, FF_NCHUNK, CONV_HALO, FF_CHUNK), F32),
                        pltpu.VMEM((tm, D_FF), BF16)],
        compiler_params=_params(("arbitrary",)),
        name="ffn",
    )(x, g, wg, wu, cwg, cwu, cbg, cbu, wd)


def _chunk_cols(w):
    r = w.shape[0]
    return w.reshape(r, FF_NCHUNK, FF_CHUNK).transpose(1, 0, 2)


def _alibi_slopes(n):
    return 2.0 ** (-8.0 * jnp.arange(1, n + 1, dtype=F32) / n)


def _block_diag_ones():
    seg = jnp.arange(NORM_SEG) // 64
    return (seg[:, None] == seg[None, :]).astype(BF16)


def _row(v):
    return v.reshape(1, -1).astype(F32)


def kernel(x, ab_norm, ab_w_in, a_q_norm, a_k_norm, a_lambda, a_sub_norm, b_w_group, b_scale, ab_w_out,
           c_norm, c_w_in, c_q_norm, c_k_norm, c_sinks, c_w_out, f_norm, f_w_up, f_conv, f_conv_b, f_w_down):
    bd = _block_diag_ones()
    h = x.reshape(N_ROWS, D_MODEL)

    a_slopes = _alibi_slopes(A_HEADS)
    ti = jnp.arange(ATT_TILE, dtype=F32)
    a_dtab = -a_slopes[:, None, None] * (ti[:, None] - ti[None, :])[None]
    c_slopes = _alibi_slopes(C_Q_HEADS)
    rel = C_QBLK + jnp.arange(C_QBLK)[:, None] - jnp.arange(2 * C_QBLK)[None, :]
    c_valid = (rel >= 0) & (rel < C_WINDOW)
    c_bias = jnp.where(c_valid[None], -c_slopes[:, None, None] * rel.astype(F32)[None], MASK_VALUE)

    for layer in range(DEPTH):
        if layer % 2 == 0:
            e = layer // 2
            lam_init = 0.8 - 0.6 * math.exp(-0.3 * layer)
            lp = a_lambda[e].astype(F32)
            lam = jnp.exp(jnp.sum(lp[0] * lp[1])) - jnp.exp(jnp.sum(lp[2] * lp[3])) + lam_init
            qg = _row(jnp.tile(a_q_norm[e], 2 * A_HEADS)) * (A_HEAD_DIM ** -0.5)
            kg = _row(jnp.tile(a_k_norm[e], 2 * A_HEADS))
            q, k, vaug, u = _ab_in(h, _row(ab_norm[e]), ab_w_in[e].astype(BF16), bd, qg, kg)
            subg = _row(a_sub_norm[e]) * (1.0 - lam_init)
            a_out = _diff_attn(a_slopes, lam.reshape(1), q, k, vaug, a_dtab, subg)
            h = _ab_out(a_out, u, h, b_w_group[e].astype(BF16), _row(b_scale[e]), ab_w_out[e].astype(BF16))
        else:
            o = layer // 2
            w = c_w_in[o]
            hd = C_HEAD_DIM
            wk = w[:, C_QW:C_QW + C_KV_HEADS * hd]
            wv = w[:, C_QW + C_KV_HEADS * hd:]
            dup = lambda m: jnp.concatenate([m[:, :hd], m[:, :hd], m[:, hd:], m[:, hd:]], axis=1)
            w_cat = jnp.concatenate([w[:, :C_QW], dup(wk), dup(wv)], axis=1).astype(BF16)
            qg = _row(jnp.tile(c_q_norm[o], C_Q_HEADS)) * (C_HEAD_DIM ** -0.5)
            kg = _row(jnp.tile(c_k_norm[o], 2 * C_KV_HEADS))
            q, kdup, vdup = _c_in(h, _row(c_norm[o]), w_cat, bd, qg, kg)
            sink = jnp.broadcast_to(c_sinks[o].astype(F32)[:, None, None], (C_Q_HEADS, 1, LANES))
            att = _swa(q, kdup, vdup, c_bias, sink)
            h = _c_out(att, h, c_w_out[o].astype(BF16))
        wup = f_w_up[layer].astype(BF16)
        cw = f_conv[layer].astype(F32)
        cb = f_conv_b[layer].astype(F32).reshape(1, -1)
        h = _ffn(h, _row(f_norm[layer]),
                 _chunk_cols(wup[:, :D_FF]), _chunk_cols(wup[:, D_FF:]),
                 _chunk_cols(cw[:, :D_FF]), _chunk_cols(cw[:, D_FF:]),
                 _chunk_cols(cb[:, :D_FF]), _chunk_cols(cb[:, D_FF:]),
                 f_w_down[layer].astype(BF16))
    return h.reshape(BATCH, SEQ, D_MODEL)
```

```python
import functools
import math

import jax
import jax.numpy as jnp
from jax import lax
from jax.experimental import pallas as pl
from jax.experimental.pallas import tpu as pltpu

D_MODEL = 1024
BATCH = 8
SEQ = 4096
DEPTH = 4
EPS = 1e-6
MASK_VALUE = -1e30

A_WIDTH = 512
A_HEAD_DIM = 64
A_HEADS = 4
A_VDIM = 2 * A_HEAD_DIM
B_WIDTH = 512
POOL_WINDOWS = (2, 4, 8, 16)
B_GROUP_DIM = 128
POOL_HALO = 16

C_HEAD_DIM = 64
C_Q_HEADS = 16
C_KV_HEADS = 2
C_GROUP = 8
C_WINDOW = 128
C_QBLK = 128
C_QW = C_Q_HEADS * C_HEAD_DIM
C_PAIRS = C_GROUP // 2

D_FF = 2816
CONV_W = 3
FF_CHUNK = 256
FF_NCHUNK = D_FF // FF_CHUNK
CONV_HALO = 8

LANES = 128
NORM_SEG = 256

ROW_TILE = 512
ATT_TILE = 256
ATT_QTILE = 2 * ATT_TILE
ATT_KCHUNK = 512
SCORE_BOUND_LIMIT = 30.0
VMEM_LIMIT = 56 * 1024 * 1024

N_ROWS = BATCH * SEQ
F32 = jnp.float32
BF16 = jnp.bfloat16


def _params(sem, vmem=VMEM_LIMIT):
    return pltpu.CompilerParams(dimension_semantics=sem, vmem_limit_bytes=vmem)


def _const_spec(shape):
    nd = len(shape)
    return pl.BlockSpec(shape, lambda *_: (0,) * nd, pipeline_mode=pl.Buffered(1))


def _rms_rows(x, g):
    ms = jnp.mean(x * x, axis=-1, keepdims=True)
    return x * lax.rsqrt(ms + EPS) * g


def _seg_norm(seg, bd, gain):
    ss = jnp.dot((seg * seg).astype(BF16), bd, preferred_element_type=F32)
    return seg * lax.rsqrt(ss * (1.0 / 64.0) + EPS) * gain


def _ab_in_kernel(x_ref, g_ref, w_ref, bd_ref, qg_ref, kg_ref, q_ref, k_ref, v_ref, u_ref):
    tm = ROW_TILE
    xn = _rms_rows(x_ref[...], g_ref[...]).astype(BF16)
    bd = bd_ref[...]
    lane = lax.broadcasted_iota(jnp.int32, (tm, LANES), 1)
    pos = (pl.program_id(0) % (SEQ // tm)) * tm + lax.broadcasted_iota(jnp.int32, (tm, LANES), 0)
    kaug = jnp.where(lane == 0, pos >> 6, jnp.where(lane == 1, pos & 63, jnp.where(lane == 2, 1, 0)))
    kaug = kaug.astype(F32).astype(BF16)
    for cb in range(A_WIDTH // NORM_SEG):
        lo, hi = cb * NORM_SEG, (cb + 1) * NORM_SEG
        qseg = jnp.dot(xn, w_ref[:, lo:hi], preferred_element_type=F32)
        q_ref[:, lo:hi] = _seg_norm(qseg, bd, qg_ref[:, lo:hi]).astype(BF16)
        kseg = jnp.dot(xn, w_ref[:, A_WIDTH + lo:A_WIDTH + hi], preferred_element_type=F32)
        kn = _seg_norm(kseg, bd, kg_ref[:, lo:hi]).astype(BF16)
        for hh in range(NORM_SEG // A_VDIM):
            h = cb * (NORM_SEG // A_VDIM) + hh
            k_ref[:, h * 2 * A_VDIM:h * 2 * A_VDIM + A_VDIM] = kn[:, hh * A_VDIM:(hh + 1) * A_VDIM]
            k_ref[:, h * 2 * A_VDIM + A_VDIM:(h + 1) * 2 * A_VDIM] = kaug
    v = jnp.dot(xn, w_ref[:, 2 * A_WIDTH:3 * A_WIDTH], preferred_element_type=F32).astype(BF16)
    ones = jnp.ones((v.shape[0], A_VDIM), BF16)
    for h in range(A_HEADS):
        v_ref[:, h * 2 * A_VDIM:h * 2 * A_VDIM + A_VDIM] = v[:, h * A_VDIM:(h + 1) * A_VDIM]
        v_ref[:, h * 2 * A_VDIM + A_VDIM:(h + 1) * 2 * A_VDIM] = ones
    u_ref[...] = jnp.dot(xn, w_ref[:, 3 * A_WIDTH:], preferred_element_type=F32)


def _ab_in(x, g, w, bd, qg, kg):
    tm = ROW_TILE
    row = lambda i: (i, 0)
    return pl.pallas_call(
        _ab_in_kernel,
        grid=(N_ROWS // tm,),
        in_specs=[pl.BlockSpec((tm, D_MODEL), row), _const_spec((1, D_MODEL)),
                  _const_spec((D_MODEL, 3 * A_WIDTH + B_WIDTH)), _const_spec((NORM_SEG, NORM_SEG)),
                  _const_spec((1, A_WIDTH)), _const_spec((1, A_WIDTH))],
        out_specs=[pl.BlockSpec((tm, A_WIDTH), row), pl.BlockSpec((tm, 2 * A_WIDTH), row),
                   pl.BlockSpec((tm, 2 * A_WIDTH), row), pl.BlockSpec((tm, B_WIDTH), row)],
        out_shape=[jax.ShapeDtypeStruct((N_ROWS, A_WIDTH), BF16), jax.ShapeDtypeStruct((N_ROWS, 2 * A_WIDTH), BF16),
                   jax.ShapeDtypeStruct((N_ROWS, 2 * A_WIDTH), BF16), jax.ShapeDtypeStruct((N_ROWS, B_WIDTH), F32)],
        compiler_params=_params(("arbitrary",)),
        name="ab_in",
    )(x, g, w, bd, qg, kg)


def _diff_attn_kernel(slope_ref, lam_ref, q_ref, k_ref, v_ref, d_ref, sg_ref, o_ref, qs_ref, m_ref, acc_ref):
    T = ATT_TILE
    h = pl.program_id(1)
    i = pl.program_id(2)
    q = q_ref[...]
    lane = lax.broadcasted_iota(jnp.int32, q.shape, 1)
    zero = jnp.zeros_like(q)
    qs_ref[0:T, :] = jnp.where(lane < A_HEAD_DIM, q, zero)
    qs_ref[T:2 * T, :] = jnp.where(lane >= A_HEAD_DIM, q, zero)
    m_ref[...] = jnp.full(m_ref.shape, MASK_VALUE, F32)
    acc_ref[...] = jnp.zeros(acc_ref.shape, F32)
    slope = slope_ref[h]
    dtab = d_ref[...]

    def step(j, masked):
        koff = pl.multiple_of(j * T, T)
        kt = k_ref[pl.ds(koff, T), :]
        vt = v_ref[pl.ds(koff, T), :]
        s = lax.dot_general(qs_ref[...], kt, (((1,), (1,)), ((), ())), preferred_element_type=F32)
        bias = dtab
        if masked:
            r = lax.broadcasted_iota(jnp.int32, (T, T), 0)
            c = lax.broadcasted_iota(jnp.int32, (T, T), 1)
            bias = jnp.where(c > r, MASK_VALUE, dtab)
        s = s + jnp.concatenate([bias, bias], axis=0)
        off = slope * ((j - i) * T).astype(F32)
        m_prev = m_ref[...]
        m_new = jnp.maximum(m_prev, jnp.max(s, axis=1, keepdims=True) + off)
        alpha = jnp.exp(m_prev - m_new)
        shift = m_new - off
        p = jnp.exp(s - jnp.concatenate([shift] * (T // LANES), axis=1))
        pv = jnp.dot(p.astype(BF16), vt, preferred_element_type=F32)
        acc_ref[...] = acc_ref[...] * jnp.concatenate([alpha, alpha], axis=1) + pv
        m_ref[...] = m_new

    def body(j, carry):
        step(j, False)
        return carry

    lax.fori_loop(0, i, body, 0)
    step(i, True)

    acc = acc_ref[...]
    o0 = acc[0:T, 0:A_VDIM] / acc[0:T, A_VDIM:]
    o1 = acc[T:, 0:A_VDIM] / acc[T:, A_VDIM:]
    o = o0 - lam_ref[0] * o1
    o_ref[...] = _rms_rows(o, sg_ref[...]).astype(BF16)


def _diff_attn_online(slopes, lam, q, kaug, vaug, dtab, subg):
    T = ATT_TILE
    nq = SEQ // T
    smem = pl.BlockSpec(memory_space=pltpu.SMEM)
    return pl.pallas_call(
        _diff_attn_kernel,
        grid=(BATCH, A_HEADS, nq),
        in_specs=[smem, smem,
                  pl.BlockSpec((T, A_VDIM), lambda b, h, i: (b * nq + i, h)),
                  pl.BlockSpec((SEQ, A_VDIM), lambda b, h, i: (b, 2 * h)),
                  pl.BlockSpec((SEQ, 2 * A_VDIM), lambda b, h, i: (b, h)),
                  pl.BlockSpec((None, T, T), lambda b, h, i: (h, 0, 0)),
                  pl.BlockSpec((1, A_VDIM), lambda b, h, i: (0, 0))],
        out_specs=pl.BlockSpec((T, A_VDIM), lambda b, h, i: (b * nq + i, h)),
        out_shape=jax.ShapeDtypeStruct((N_ROWS, A_WIDTH), BF16),
        scratch_shapes=[pltpu.VMEM((2 * T, A_VDIM), BF16), pltpu.VMEM((2 * T, LANES), F32),
                        pltpu.VMEM((2 * T, 2 * A_VDIM), F32)],
        compiler_params=_params(("arbitrary", "arbitrary", "arbitrary")),
        name="diff_attn_online",
    )(slopes, lam, q, kaug, vaug, dtab, subg)


def _diff_attn_bounded_kernel(slope_ref, lam_ref, bound_ref, q_ref, k_ref, v_ref, sg_ref, o_ref,
                              qs_ref, acc_ref, s_ref):
    TQ, HB, C = ATT_QTILE, ATT_TILE, ATT_KCHUNK
    h = pl.program_id(1)
    i = pl.program_id(2)
    slope = slope_ref[h]
    q = q_ref[...]
    lane = lax.broadcasted_iota(jnp.int32, q.shape, 1)
    t = (i * TQ + lax.broadcasted_iota(jnp.int32, q.shape, 0)).astype(F32)
    row_shift = -(slope * t + bound_ref[0])
    aug = jnp.where(lane == 0, 64.0 * slope, jnp.where(lane == 1, slope, jnp.where(lane == 2, row_shift, 0.0)))
    aug = aug.astype(BF16)
    zero = jnp.zeros_like(q)
    maps = (jnp.where(lane < A_HEAD_DIM, q, zero), jnp.where(lane >= A_HEAD_DIM, q, zero))
    for blk in range(TQ // HB):
        for mp in range(2):
            r0 = (2 * blk + mp) * HB
            qs_ref[r0:r0 + HB, 0:A_VDIM] = maps[mp][blk * HB:(blk + 1) * HB, :]
            qs_ref[r0:r0 + HB, A_VDIM:] = aug[blk * HB:(blk + 1) * HB, :]
    acc_ref[...] = jnp.zeros(acc_ref.shape, F32)

    def scores(j):
        off = pl.multiple_of(j * C, C)
        return lax.dot_general(qs_ref[...], k_ref[pl.ds(off, C), :], (((1,), (1,)), ((), ())),
                               preferred_element_type=F32)

    def accumulate(j, s):
        off = pl.multiple_of(j * C, C)
        acc_ref[...] += jnp.dot(jnp.exp(s).astype(BF16), v_ref[pl.ds(off, C), :], preferred_element_type=F32)

    s_ref[0] = scores(0)

    def body(j, carry):
        s_cur = s_ref[j % 2]
        s_ref[(j + 1) % 2] = scores(j + 1)
        accumulate(j, s_cur)
        return carry

    lax.fori_loop(0, i, body, 0)

    r = lax.broadcasted_iota(jnp.int32, (HB, HB), 0)
    c = lax.broadcasted_iota(jnp.int32, (HB, HB), 1)
    tri = jnp.where(c > r, MASK_VALUE, 0.0).astype(F32)
    tri2 = jnp.concatenate([tri, tri], axis=0)
    hidden = jnp.full((2 * HB, HB), MASK_VALUE, F32)
    visible = jnp.zeros((2 * HB, HB), F32)
    mask = jnp.concatenate([jnp.concatenate([tri2, hidden], axis=1),
                            jnp.concatenate([visible, tri2], axis=1)], axis=0)
    accumulate(i, s_ref[i % 2] + mask)

    lam = lam_ref[0]
    for blk in range(TQ // HB):
        a0 = acc_ref[(2 * blk) * HB:(2 * blk + 1) * HB, :]
        a1 = acc_ref[(2 * blk + 1) * HB:(2 * blk + 2) * HB, :]
        o = a0[:, 0:A_VDIM] / a0[:, A_VDIM:] - lam * (a1[:, 0:A_VDIM] / a1[:, A_VDIM:])
        o_ref[blk * HB:(blk + 1) * HB, :] = _rms_rows(o, sg_ref[...]).astype(BF16)


def _diff_attn_bounded(slopes, lam, bound, q, kaug, vaug, subg):
    TQ = ATT_QTILE
    nq = SEQ // TQ
    smem = pl.BlockSpec(memory_space=pltpu.SMEM)
    return pl.pallas_call(
        _diff_attn_bounded_kernel,
        grid=(BATCH, A_HEADS, nq),
        in_specs=[smem, smem, smem,
                  pl.BlockSpec((TQ, A_VDIM), lambda b, h, i: (b * nq + i, h)),
                  pl.BlockSpec((SEQ, 2 * A_VDIM), lambda b, h, i: (b, h)),
                  pl.BlockSpec((SEQ, 2 * A_VDIM), lambda b, h, i: (b, h)),
                  pl.BlockSpec((1, A_VDIM), lambda b, h, i: (0, 0))],
        out_specs=pl.BlockSpec((TQ, A_VDIM), lambda b, h, i: (b * nq + i, h)),
        out_shape=jax.ShapeDtypeStruct((N_ROWS, A_WIDTH), BF16),
        scratch_shapes=[pltpu.VMEM((2 * TQ, 2 * A_VDIM), BF16), pltpu.VMEM((2 * TQ, 2 * A_VDIM), F32),
                        pltpu.VMEM((2, 2 * TQ, ATT_KCHUNK), F32)],
        compiler_params=_params(("arbitrary", "arbitrary", "arbitrary")),
        name="diff_attn_bounded",
    )(slopes, lam, bound, q, kaug, vaug, subg)


def _ab_out_kernel(a_ref, u_ref, x_ref, wg_ref, ps_ref, wo_ref, o_ref, uext_ref, carry_ref):
    tm = ROW_TILE
    tiles_per_seq = SEQ // tm
    it = pl.program_id(0) % tiles_per_seq

    @pl.when(it == 0)
    def _():
        carry_ref[...] = jnp.zeros(carry_ref.shape, F32)

    u = u_ref[...]
    uext_ref[0:POOL_HALO, :] = carry_ref[...]
    uext_ref[POOL_HALO:, :] = u
    carry_ref[...] = u[tm - POOL_HALO:, :]
    pos = it * tm + lax.broadcasted_iota(jnp.int32, (tm, 1), 0)
    mixed = []
    for g, w in enumerate(POOL_WINDOWS):
        lo, hi = g * B_GROUP_DIM, (g + 1) * B_GROUP_DIM
        ug = u[:, lo:hi]
        win = ug
        for d in range(1, w):
            win = win + uext_ref[pl.ds(POOL_HALO - d, tm), lo:hi]
        cnt = jnp.minimum(pos + 1, w).astype(F32)
        pooled = win / cnt - ug
        mg = jnp.dot(pooled.astype(BF16), wg_ref[g], preferred_element_type=F32)
        mixed.append((mg * ps_ref[:, lo:hi]).astype(BF16))
    mix = jnp.concatenate([a_ref[...]] + mixed, axis=1)
    o_ref[...] = x_ref[...] + jnp.dot(mix, wo_ref[...], preferred_element_type=F32)


def _ab_out(a, u, x, wg, ps, wo):
    tm = ROW_TILE
    row = lambda i: (i, 0)
    return pl.pallas_call(
        _ab_out_kernel,
        grid=(N_ROWS // tm,),
        in_specs=[pl.BlockSpec((tm, A_WIDTH), row), pl.BlockSpec((tm, B_WIDTH), row),
                  pl.BlockSpec((tm, D_MODEL), row), _const_spec((len(POOL_WINDOWS), B_GROUP_DIM, B_GROUP_DIM)),
                  _const_spec((1, B_WIDTH)), _const_spec((A_WIDTH + B_WIDTH, D_MODEL))],
        out_specs=pl.BlockSpec((tm, D_MODEL), row),
        out_shape=jax.ShapeDtypeStruct((N_ROWS, D_MODEL), F32),
        scratch_shapes=[pltpu.VMEM((tm + POOL_HALO, B_WIDTH), F32), pltpu.VMEM((POOL_HALO, B_WIDTH), F32)],
        compiler_params=_params(("arbitrary",)),
        name="ab_out",
    )(a, u, x, wg, ps, wo)


def _c_in_kernel(x_ref, g_ref, w_ref, bd_ref, qg_ref, kg_ref, q_ref, k_ref, v_ref):
    xn = _rms_rows(x_ref[...], g_ref[...]).astype(BF16)
    bd = bd_ref[...]
    for cb in range(C_QW // NORM_SEG):
        lo, hi = cb * NORM_SEG, (cb + 1) * NORM_SEG
        qseg = jnp.dot(xn, w_ref[:, lo:hi], preferred_element_type=F32)
        q_ref[:, lo:hi] = _seg_norm(qseg, bd, qg_ref[:, lo:hi]).astype(BF16)
    kseg = jnp.dot(xn, w_ref[:, C_QW:C_QW + NORM_SEG], preferred_element_type=F32)
    k_ref[...] = _seg_norm(kseg, bd, kg_ref[...]).astype(BF16)
    v_ref[...] = jnp.dot(xn, w_ref[:, C_QW + NORM_SEG:], preferred_element_type=F32).astype(BF16)


def _c_in(x, g, w, bd, qg, kg):
    tm = ROW_TILE
    row = lambda i: (i, 0)
    kvw = 2 * C_KV_HEADS * C_HEAD_DIM
    return pl.pallas_call(
        _c_in_kernel,
        grid=(N_ROWS // tm,),
        in_specs=[pl.BlockSpec((tm, D_MODEL), row), _const_spec((1, D_MODEL)),
                  _const_spec((D_MODEL, C_QW + 2 * kvw)), _const_spec((NORM_SEG, NORM_SEG)),
                  _const_spec((1, C_QW)), _const_spec((1, kvw))],
        out_specs=[pl.BlockSpec((tm, C_QW), row), pl.BlockSpec((tm, kvw), row), pl.BlockSpec((tm, kvw), row)],
        out_shape=[jax.ShapeDtypeStruct((N_ROWS, C_QW), BF16), jax.ShapeDtypeStruct((N_ROWS, kvw), BF16),
                   jax.ShapeDtypeStruct((N_ROWS, kvw), BF16)],
        compiler_params=_params(("arbitrary",)),
        name="c_in",
    )(x, g, w, bd, qg, kg)


def _swa_kernel(q_ref, kp_ref, kc_ref, vp_ref, vc_ref, bias_ref, sink_ref, o_ref):
    Q = C_QBLK
    i = pl.program_id(1)
    q = q_ref[...]
    lane = lax.broadcasted_iota(jnp.int32, (Q, LANES), 1)
    first_half = lane < C_HEAD_DIM
    zero = jnp.zeros((Q, LANES), BF16)
    rows = []
    for p in range(C_PAIRS):
        qp = q[:, p * LANES:(p + 1) * LANES]
        rows.append(jnp.where(first_half, qp, zero))
        rows.append(jnp.where(first_half, zero, qp))
    qs = jnp.concatenate(rows, axis=0)
    k2 = jnp.concatenate([kp_ref[...], kc_ref[...]], axis=0)
    v2 = jnp.concatenate([vp_ref[...], vc_ref[...]], axis=0)
    s = lax.dot_general(qs, k2, (((1,), (1,)), ((), ())), preferred_element_type=F32)
    s = s.reshape(C_GROUP, Q, 2 * Q) + bias_ref[...]
    col = lax.broadcasted_iota(jnp.int32, (C_GROUP, Q, 2 * Q), 2)
    s = jnp.where(jnp.logical_and(i == 0, col < Q), MASK_VALUE, s)
    sink = sink_ref[...][:, :, 0:1]
    m = jnp.maximum(jnp.max(s, axis=-1, keepdims=True), sink)
    p = jnp.exp(s - m)
    denom = jnp.sum(p, axis=-1, keepdims=True) + jnp.exp(sink - m)
    pn = (p / denom).reshape(C_GROUP * Q, 2 * Q).astype(BF16)
    o = jnp.dot(pn, v2, preferred_element_type=F32)
    for p_ in range(C_PAIRS):
        oa = o[(2 * p_) * Q:(2 * p_ + 1) * Q, :]
        ob = o[(2 * p_ + 1) * Q:(2 * p_ + 2) * Q, :]
        o_ref[:, p_ * LANES:(p_ + 1) * LANES] = jnp.where(first_half, oa, ob).astype(BF16)


def _swa(q, kdup, vdup, bias, sink):
    Q = C_QBLK
    nb = SEQ // Q
    gw = C_GROUP * C_HEAD_DIM
    cur = lambda b, i, g: (b * nb + i, g)
    prev = lambda b, i, g: (b * nb + jnp.maximum(i - 1, 0), g)
    return pl.pallas_call(
        _swa_kernel,
        grid=(BATCH, nb, C_KV_HEADS),
        in_specs=[pl.BlockSpec((Q, gw), cur),
                  pl.BlockSpec((Q, LANES), prev), pl.BlockSpec((Q, LANES), cur),
                  pl.BlockSpec((Q, LANES), prev), pl.BlockSpec((Q, LANES), cur),
                  pl.BlockSpec((C_GROUP, Q, 2 * Q), lambda b, i, g: (g, 0, 0)),
                  pl.BlockSpec((C_GROUP, 1, LANES), lambda b, i, g: (g, 0, 0))],
        out_specs=pl.BlockSpec((Q, gw), cur),
        out_shape=jax.ShapeDtypeStruct((N_ROWS, C_QW), BF16),
        compiler_params=_params(("arbitrary", "arbitrary", "arbitrary")),
        name="swa",
    )(q, kdup, kdup, vdup, vdup, bias, sink)


def _c_out_kernel(o_ref, x_ref, w_ref, y_ref):
    y_ref[...] = x_ref[...] + jnp.dot(o_ref[...], w_ref[...], preferred_element_type=F32)


def _c_out(o, x, w):
    tm = ROW_TILE
    row = lambda i: (i, 0)
    return pl.pallas_call(
        _c_out_kernel,
        grid=(N_ROWS // tm,),
        in_specs=[pl.BlockSpec((tm, C_QW), row), pl.BlockSpec((tm, D_MODEL), row), _const_spec((C_QW, D_MODEL))],
        out_specs=pl.BlockSpec((tm, D_MODEL), row),
        out_shape=jax.ShapeDtypeStruct((N_ROWS, D_MODEL), F32),
        compiler_params=_params(("arbitrary",)),
        name="c_out",
    )(o, x, w)


def _ffn_kernel(x_ref, g_ref, wg_ref, wu_ref, cwg_ref, cwu_ref, cbg_ref, cbu_ref, wd_ref, o_ref,
                hs_ref, carry_ref, act_ref):
    tm = ROW_TILE
    H = CONV_HALO

    @pl.when(pl.program_id(0) % (SEQ // tm) == 0)
    def _():
        carry_ref[...] = jnp.zeros(carry_ref.shape, F32)

    x = x_ref[...]
    xn = _rms_rows(x, g_ref[...]).astype(BF16)

    def conv(c, part, w_ref, cw_ref, cb_ref):
        h = jnp.dot(xn, w_ref[c], preferred_element_type=F32)
        buf = hs_ref.at[c % 2, part]
        buf[0:H, :] = carry_ref[part, c]
        buf[H:, :] = h
        carry_ref[part, c] = h[tm - H:, :]
        cw = cw_ref[c]
        return (cb_ref[c] + cw[2:3, :] * h + cw[1:2, :] * buf[pl.ds(H - 1, tm), :]
                + cw[0:1, :] * buf[pl.ds(H - 2, tm), :])

    for c in range(FF_NCHUNK):
        gate = conv(c, 0, wg_ref, cwg_ref, cbg_ref)
        up = conv(c, 1, wu_ref, cwu_ref, cbu_ref)
        act = gate / (1.0 + jnp.exp(-gate)) * up
        act_ref[:, c * FF_CHUNK:(c + 1) * FF_CHUNK] = act.astype(BF16)
    o_ref[...] = x + jnp.dot(act_ref[...], wd_ref[...], preferred_element_type=F32)


def _ffn(x, g, wg, wu, cwg, cwu, cbg, cbu, wd):
    tm = ROW_TILE
    row = lambda i: (i, 0)
    return pl.pallas_call(
        _ffn_kernel,
        grid=(N_ROWS // tm,),
        in_specs=[pl.BlockSpec((tm, D_MODEL), row), _const_spec((1, D_MODEL)),
                  _const_spec((FF_NCHUNK, D_MODEL, FF_CHUNK)), _const_spec((FF_NCHUNK, D_MODEL, FF_CHUNK)),
                  _const_spec((FF_NCHUNK, CONV_W, FF_CHUNK)), _const_spec((FF_NCHUNK, CONV_W, FF_CHUNK)),
                  _const_spec((FF_NCHUNK, 1, FF_CHUNK)), _const_spec((FF_NCHUNK, 1, FF_CHUNK)),
                  _const_spec((D_FF, D_MODEL))],
        out_specs=pl.BlockSpec((tm, D_MODEL), row),
        out_shape=jax.ShapeDtypeStruct((N_ROWS, D_MODEL), F32),
        scratch_shapes=[pltpu.VMEM((2, 2, tm + CONV_HALO, FF_CHUNK), F32),
                        pltpu.VMEM((2, FF_NCHUNK, CONV_HALO, FF_CHUNK), F32),
                        pltpu.VMEM((tm, D_FF), BF16)],
        compiler_params=_params(("arbitrary",)),
        name="ffn",
    )(x, g, wg, wu, cwg, cwu, cbg, cbu, wd)


def _chunk_cols(w):
    r = w.shape[0]
    return w.reshape(r, FF_NCHUNK, FF_CHUNK).transpose(1, 0, 2)


def _alibi_slopes(n):
    return 2.0 ** (-8.0 * jnp.arange(1, n + 1, dtype=F32) / n)


def _block_diag_ones():
    seg = jnp.arange(NORM_SEG) // 64
    return (seg[:, None] == seg[None, :]).astype(BF16)


def _row(v):
    return v.reshape(1, -1).astype(F32)


def kernel(x, ab_norm, ab_w_in, a_q_norm, a_k_norm, a_lambda, a_sub_norm, b_w_group, b_scale, ab_w_out,
           c_norm, c_w_in, c_q_norm, c_k_norm, c_sinks, c_w_out, f_norm, f_w_up, f_conv, f_conv_b, f_w_down):
    bd = _block_diag_ones()
    h = x.reshape(N_ROWS, D_MODEL)

    a_slopes = _alibi_slopes(A_HEADS)
    ti = jnp.arange(ATT_TILE, dtype=F32)
    a_dtab = -a_slopes[:, None, None] * (ti[:, None] - ti[None, :])[None]
    c_slopes = _alibi_slopes(C_Q_HEADS)
    rel = C_QBLK + jnp.arange(C_QBLK)[:, None] - jnp.arange(2 * C_QBLK)[None, :]
    c_valid = (rel >= 0) & (rel < C_WINDOW)
    c_bias = jnp.where(c_valid[None], -c_slopes[:, None, None] * rel.astype(F32)[None], MASK_VALUE)

    for layer in range(DEPTH):
        if layer % 2 == 0:
            e = layer // 2
            lam_init = 0.8 - 0.6 * math.exp(-0.3 * layer)
            lp = a_lambda[e].astype(F32)
            lam = jnp.exp(jnp.sum(lp[0] * lp[1])) - jnp.exp(jnp.sum(lp[2] * lp[3])) + lam_init
            qg = _row(jnp.tile(a_q_norm[e], 2 * A_HEADS)) * (A_HEAD_DIM ** -0.5)
            kg = _row(jnp.tile(a_k_norm[e], 2 * A_HEADS))
            q, kaug, vaug, u = _ab_in(h, _row(ab_norm[e]), ab_w_in[e].astype(BF16), bd, qg, kg)
            subg = _row(a_sub_norm[e]) * (1.0 - lam_init)
            lam1 = lam.reshape(1)
            bound = (1.02 * A_HEAD_DIM ** 0.5) * jnp.max(jnp.abs(a_q_norm[e])) * jnp.max(jnp.abs(a_k_norm[e]))
            bound = bound.astype(F32).reshape(1)
            a_out = lax.cond(
                bound[0] <= SCORE_BOUND_LIMIT,
                lambda: _diff_attn_bounded(a_slopes, lam1, bound, q, kaug, vaug, subg),
                lambda: _diff_attn_online(a_slopes, lam1, q, kaug, vaug, a_dtab, subg))
            h = _ab_out(a_out, u, h, b_w_group[e].astype(BF16), _row(b_scale[e]), ab_w_out[e].astype(BF16))
        else:
            o = layer // 2
            w = c_w_in[o]
            hd = C_HEAD_DIM
            wk = w[:, C_QW:C_QW + C_KV_HEADS * hd]
            wv = w[:, C_QW + C_KV_HEADS * hd:]
            dup = lambda m: jnp.concatenate([m[:, :hd], m[:, :hd], m[:, hd:], m[:, hd:]], axis=1)
            w_cat = jnp.concatenate([w[:, :C_QW], dup(wk), dup(wv)], axis=1).astype(BF16)
            qg = _row(jnp.tile(c_q_norm[o], C_Q_HEADS)) * (C_HEAD_DIM ** -0.5)
            kg = _row(jnp.tile(c_k_norm[o], 2 * C_KV_HEADS))
            q, kdup, vdup = _c_in(h, _row(c_norm[o]), w_cat, bd, qg, kg)
            sink = jnp.broadcast_to(c_sinks[o].astype(F32)[:, None, None], (C_Q_HEADS, 1, LANES))
            att = _swa(q, kdup, vdup, c_bias, sink)
            h = _c_out(att, h, c_w_out[o].astype(BF16))
        wup = f_w_up[layer].astype(BF16)
        cw = f_conv[layer].astype(F32)
        cb = f_conv_b[layer].astype(F32).reshape(1, -1)
        h = _ffn(h, _row(f_norm[layer]),
                 _chunk_cols(wup[:, :D_FF]), _chunk_cols(wup[:, D_FF:]),
                 _chunk_cols(cw[:, :D_FF]), _chunk_cols(cw[:, D_FF:]),
                 _chunk_cols(cb[:, :D_FF]), _chunk_cols(cb[:, D_FF:]),
                 f_w_down[layer].astype(BF16))
    return h.reshape(BATCH, SEQ, D_MODEL)
```

```python
import functools
import math

import jax
import jax.numpy as jnp
from jax import lax
from jax.experimental import pallas as pl
from jax.experimental.pallas import tpu as pltpu

D_MODEL = 1024
BATCH = 8
SEQ = 4096
DEPTH = 4
EPS = 1e-6
MASK_VALUE = -1e30

A_WIDTH = 512
A_HEAD_DIM = 64
A_HEADS = 4
A_VDIM = 2 * A_HEAD_DIM
B_WIDTH = 512
POOL_WINDOWS = (2, 4, 8, 16)
B_GROUP_DIM = 128
POOL_HALO = 16

C_HEAD_DIM = 64
C_Q_HEADS = 16
C_KV_HEADS = 2
C_GROUP = 8
C_WINDOW = 128
C_QBLK = 128
C_QW = C_Q_HEADS * C_HEAD_DIM
C_PAIRS = C_GROUP // 2

D_FF = 2816
CONV_W = 3
FF_CHUNK = 256
FF_NCHUNK = D_FF // FF_CHUNK
CONV_HALO = 8

LANES = 128
NORM_SEG = 256

ROW_TILE = 512
ATT_TILE = 256
ATT_QTILE = 2 * ATT_TILE
ATT_KCHUNK = 512
SCORE_BOUND_LIMIT = 30.0
VMEM_LIMIT = 56 * 1024 * 1024

N_ROWS = BATCH * SEQ
F32 = jnp.float32
BF16 = jnp.bfloat16


def _params(sem, vmem=VMEM_LIMIT):
    return pltpu.CompilerParams(dimension_semantics=sem, vmem_limit_bytes=vmem)


def _const_spec(shape):
    nd = len(shape)
    return pl.BlockSpec(shape, lambda *_: (0,) * nd, pipeline_mode=pl.Buffered(1))


def _rms_rows(x, g):
    ms = jnp.mean(x * x, axis=-1, keepdims=True)
    return x * lax.rsqrt(ms + EPS) * g


def _seg_norm(seg, bd, gain):
    ss = jnp.dot((seg * seg).astype(BF16), bd, preferred_element_type=F32)
    return seg * lax.rsqrt(ss * (1.0 / 64.0) + EPS) * gain


def _ab_in_kernel(x_ref, g_ref, w_ref, bd_ref, qg_ref, kg_ref, q_ref, k_ref, v_ref, u_ref):
    tm = ROW_TILE
    xn = _rms_rows(x_ref[...], g_ref[...]).astype(BF16)
    bd = bd_ref[...]
    lane = lax.broadcasted_iota(jnp.int32, (tm, LANES), 1)
    pos = (pl.program_id(0) % (SEQ // tm)) * tm + lax.broadcasted_iota(jnp.int32, (tm, LANES), 0)
    kaug = jnp.where(lane == 0, pos >> 6, jnp.where(lane == 1, pos & 63, jnp.where(lane == 2, 1, 0)))
    kaug = kaug.astype(F32).astype(BF16)
    def finish(seg, is_q, cb):
        lo, hi = cb * NORM_SEG, (cb + 1) * NORM_SEG
        if is_q:
            q_ref[:, lo:hi] = _seg_norm(seg, bd, qg_ref[:, lo:hi]).astype(BF16)
            return
        kn = _seg_norm(seg, bd, kg_ref[:, lo:hi]).astype(BF16)
        for hh in range(NORM_SEG // A_VDIM):
            h = cb * (NORM_SEG // A_VDIM) + hh
            k_ref[:, h * 2 * A_VDIM:h * 2 * A_VDIM + A_VDIM] = kn[:, hh * A_VDIM:(hh + 1) * A_VDIM]
            k_ref[:, h * 2 * A_VDIM + A_VDIM:(h + 1) * 2 * A_VDIM] = kaug

    pending = None
    for cb in range(A_WIDTH // NORM_SEG):
        for is_q in (True, False):
            c0 = cb * NORM_SEG + (0 if is_q else A_WIDTH)
            seg = jnp.dot(xn, w_ref[:, c0:c0 + NORM_SEG], preferred_element_type=F32)
            if pending is not None:
                finish(*pending)
            pending = (seg, is_q, cb)
    v = jnp.dot(xn, w_ref[:, 2 * A_WIDTH:3 * A_WIDTH], preferred_element_type=F32).astype(BF16)
    finish(*pending)
    ones = jnp.ones((v.shape[0], A_VDIM), BF16)
    for h in range(A_HEADS):
        v_ref[:, h * 2 * A_VDIM:h * 2 * A_VDIM + A_VDIM] = v[:, h * A_VDIM:(h + 1) * A_VDIM]
        v_ref[:, h * 2 * A_VDIM + A_VDIM:(h + 1) * 2 * A_VDIM] = ones
    u_ref[...] = jnp.dot(xn, w_ref[:, 3 * A_WIDTH:], preferred_element_type=F32)


def _ab_in(x, g, w, bd, qg, kg):
    tm = ROW_TILE
    row = lambda i: (i, 0)
    return pl.pallas_call(
        _ab_in_kernel,
        grid=(N_ROWS // tm,),
        in_specs=[pl.BlockSpec((tm, D_MODEL), row), _const_spec((1, D_MODEL)),
                  _const_spec((D_MODEL, 3 * A_WIDTH + B_WIDTH)), _const_spec((NORM_SEG, NORM_SEG)),
                  _const_spec((1, A_WIDTH)), _const_spec((1, A_WIDTH))],
        out_specs=[pl.BlockSpec((tm, A_WIDTH), row), pl.BlockSpec((tm, 2 * A_WIDTH), row),
                   pl.BlockSpec((tm, 2 * A_WIDTH), row), pl.BlockSpec((tm, B_WIDTH), row)],
        out_shape=[jax.ShapeDtypeStruct((N_ROWS, A_WIDTH), BF16), jax.ShapeDtypeStruct((N_ROWS, 2 * A_WIDTH), BF16),
                   jax.ShapeDtypeStruct((N_ROWS, 2 * A_WIDTH), BF16), jax.ShapeDtypeStruct((N_ROWS, B_WIDTH), F32)],
        compiler_params=_params(("arbitrary",)),
        name="ab_in",
    )(x, g, w, bd, qg, kg)


def _diff_attn_kernel(slope_ref, lam_ref, q_ref, k_ref, v_ref, d_ref, sg_ref, o_ref, qs_ref, m_ref, acc_ref):
    T = ATT_TILE
    h = pl.program_id(1)
    i = pl.program_id(2)
    q = q_ref[...]
    lane = lax.broadcasted_iota(jnp.int32, q.shape, 1)
    zero = jnp.zeros_like(q)
    qs_ref[0:T, :] = jnp.where(lane < A_HEAD_DIM, q, zero)
    qs_ref[T:2 * T, :] = jnp.where(lane >= A_HEAD_DIM, q, zero)
    m_ref[...] = jnp.full(m_ref.shape, MASK_VALUE, F32)
    acc_ref[...] = jnp.zeros(acc_ref.shape, F32)
    slope = slope_ref[h]
    dtab = d_ref[...]

    def step(j, masked):
        koff = pl.multiple_of(j * T, T)
        kt = k_ref[pl.ds(koff, T), :]
        vt = v_ref[pl.ds(koff, T), :]
        s = lax.dot_general(qs_ref[...], kt, (((1,), (1,)), ((), ())), preferred_element_type=F32)
        bias = dtab
        if masked:
            r = lax.broadcasted_iota(jnp.int32, (T, T), 0)
            c = lax.broadcasted_iota(jnp.int32, (T, T), 1)
            bias = jnp.where(c > r, MASK_VALUE, dtab)
        s = s + jnp.concatenate([bias, bias], axis=0)
        off = slope * ((j - i) * T).astype(F32)
        m_prev = m_ref[...]
        m_new = jnp.maximum(m_prev, jnp.max(s, axis=1, keepdims=True) + off)
        alpha = jnp.exp(m_prev - m_new)
        shift = m_new - off
        p = jnp.exp(s - jnp.concatenate([shift] * (T // LANES), axis=1))
        pv = jnp.dot(p.astype(BF16), vt, preferred_element_type=F32)
        acc_ref[...] = acc_ref[...] * jnp.concatenate([alpha, alpha], axis=1) + pv
        m_ref[...] = m_new

    def body(j, carry):
        step(j, False)
        return carry

    lax.fori_loop(0, i, body, 0)
    step(i, True)

    acc = acc_ref[...]
    o0 = acc[0:T, 0:A_VDIM] / acc[0:T, A_VDIM:]
    o1 = acc[T:, 0:A_VDIM] / acc[T:, A_VDIM:]
    o = o0 - lam_ref[0] * o1
    o_ref[...] = _rms_rows(o, sg_ref[...]).astype(BF16)


def _diff_attn_online(slopes, lam, q, kaug, vaug, dtab, subg):
    T = ATT_TILE
    nq = SEQ // T
    smem = pl.BlockSpec(memory_space=pltpu.SMEM)
    return pl.pallas_call(
        _diff_attn_kernel,
        grid=(BATCH, A_HEADS, nq),
        in_specs=[smem, smem,
                  pl.BlockSpec((T, A_VDIM), lambda b, h, i: (b * nq + i, h)),
                  pl.BlockSpec((SEQ, A_VDIM), lambda b, h, i: (b, 2 * h)),
                  pl.BlockSpec((SEQ, 2 * A_VDIM), lambda b, h, i: (b, h)),
                  pl.BlockSpec((None, T, T), lambda b, h, i: (h, 0, 0)),
                  pl.BlockSpec((1, A_VDIM), lambda b, h, i: (0, 0))],
        out_specs=pl.BlockSpec((T, A_VDIM), lambda b, h, i: (b * nq + i, h)),
        out_shape=jax.ShapeDtypeStruct((N_ROWS, A_WIDTH), BF16),
        scratch_shapes=[pltpu.VMEM((2 * T, A_VDIM), BF16), pltpu.VMEM((2 * T, LANES), F32),
                        pltpu.VMEM((2 * T, 2 * A_VDIM), F32)],
        compiler_params=_params(("arbitrary", "arbitrary", "arbitrary")),
        name="diff_attn_online",
    )(slopes, lam, q, kaug, vaug, dtab, subg)


def _diff_attn_bounded_kernel(slope_ref, lam_ref, bound_ref, q_ref, k_ref, v_ref, sg_ref, o_ref,
                              qs_ref, acc_ref, s_ref):
    TQ, HB, C = ATT_QTILE, ATT_TILE, ATT_KCHUNK
    h = pl.program_id(1)
    i = pl.program_id(2)
    slope = slope_ref[h]
    q = q_ref[...]
    lane = lax.broadcasted_iota(jnp.int32, q.shape, 1)
    t = (i * TQ + lax.broadcasted_iota(jnp.int32, q.shape, 0)).astype(F32)
    row_shift = -(slope * t + bound_ref[0])
    aug = jnp.where(lane == 0, 64.0 * slope, jnp.where(lane == 1, slope, jnp.where(lane == 2, row_shift, 0.0)))
    aug = aug.astype(BF16)
    zero = jnp.zeros_like(q)
    maps = (jnp.where(lane < A_HEAD_DIM, q, zero), jnp.where(lane >= A_HEAD_DIM, q, zero))
    for blk in range(TQ // HB):
        for mp in range(2):
            r0 = (2 * blk + mp) * HB
            qs_ref[r0:r0 + HB, 0:A_VDIM] = maps[mp][blk * HB:(blk + 1) * HB, :]
            qs_ref[r0:r0 + HB, A_VDIM:] = aug[blk * HB:(blk + 1) * HB, :]
    acc_ref[...] = jnp.zeros(acc_ref.shape, F32)

    def scores(j):
        off = pl.multiple_of(j * C, C)
        return lax.dot_general(qs_ref[...], k_ref[pl.ds(off, C), :], (((1,), (1,)), ((), ())),
                               preferred_element_type=F32)

    def accumulate(j, s):
        off = pl.multiple_of(j * C, C)
        acc_ref[...] += jnp.dot(jnp.exp(s).astype(BF16), v_ref[pl.ds(off, C), :], preferred_element_type=F32)

    s_ref[0] = scores(0)

    def body(j, carry):
        s_cur = s_ref[j % 2]
        s_ref[(j + 1) % 2] = scores(j + 1)
        accumulate(j, s_cur)
        return carry

    lax.fori_loop(0, i, body, 0)

    r = lax.broadcasted_iota(jnp.int32, (HB, HB), 0)
    c = lax.broadcasted_iota(jnp.int32, (HB, HB), 1)
    tri = jnp.where(c > r, MASK_VALUE, 0.0).astype(F32)
    tri2 = jnp.concatenate([tri, tri], axis=0)
    hidden = jnp.full((2 * HB, HB), MASK_VALUE, F32)
    visible = jnp.zeros((2 * HB, HB), F32)
    mask = jnp.concatenate([jnp.concatenate([tri2, hidden], axis=1),
                            jnp.concatenate([visible, tri2], axis=1)], axis=0)
    accumulate(i, s_ref[i % 2] + mask)

    lam = lam_ref[0]
    for blk in range(TQ // HB):
        a0 = acc_ref[(2 * blk) * HB:(2 * blk + 1) * HB, :]
        a1 = acc_ref[(2 * blk + 1) * HB:(2 * blk + 2) * HB, :]
        o = a0[:, 0:A_VDIM] / a0[:, A_VDIM:] - lam * (a1[:, 0:A_VDIM] / a1[:, A_VDIM:])
        o_ref[blk * HB:(blk + 1) * HB, :] = _rms_rows(o, sg_ref[...]).astype(BF16)


def _diff_attn_bounded(slopes, lam, bound, q, kaug, vaug, subg):
    TQ = ATT_QTILE
    nq = SEQ // TQ
    smem = pl.BlockSpec(memory_space=pltpu.SMEM)
    return pl.pallas_call(
        _diff_attn_bounded_kernel,
        grid=(BATCH, A_HEADS, nq),
        in_specs=[smem, smem, smem,
                  pl.BlockSpec((TQ, A_VDIM), lambda b, h, i: (b * nq + i, h)),
                  pl.BlockSpec((SEQ, 2 * A_VDIM), lambda b, h, i: (b, h)),
                  pl.BlockSpec((SEQ, 2 * A_VDIM), lambda b, h, i: (b, h)),
                  pl.BlockSpec((1, A_VDIM), lambda b, h, i: (0, 0))],
        out_specs=pl.BlockSpec((TQ, A_VDIM), lambda b, h, i: (b * nq + i, h)),
        out_shape=jax.ShapeDtypeStruct((N_ROWS, A_WIDTH), BF16),
        scratch_shapes=[pltpu.VMEM((2 * TQ, 2 * A_VDIM), BF16), pltpu.VMEM((2 * TQ, 2 * A_VDIM), F32),
                        pltpu.VMEM((2, 2 * TQ, ATT_KCHUNK), F32)],
        compiler_params=_params(("arbitrary", "arbitrary", "arbitrary")),
        name="diff_attn_bounded",
    )(slopes, lam, bound, q, kaug, vaug, subg)


def _ab_out_kernel(a_ref, u_ref, x_ref, wg_ref, ps_ref, wo_ref, o_ref, uext_ref, carry_ref):
    tm = ROW_TILE
    tiles_per_seq = SEQ // tm
    it = pl.program_id(0) % tiles_per_seq

    @pl.when(it == 0)
    def _():
        carry_ref[...] = jnp.zeros(carry_ref.shape, F32)

    u = u_ref[...]
    uext_ref[0:POOL_HALO, :] = carry_ref[...]
    uext_ref[POOL_HALO:, :] = u
    carry_ref[...] = u[tm - POOL_HALO:, :]
    pos = it * tm + lax.broadcasted_iota(jnp.int32, (tm, 1), 0)
    mixed = []
    for g, w in enumerate(POOL_WINDOWS):
        lo, hi = g * B_GROUP_DIM, (g + 1) * B_GROUP_DIM
        ug = u[:, lo:hi]
        win = ug
        for d in range(1, w):
            win = win + uext_ref[pl.ds(POOL_HALO - d, tm), lo:hi]
        cnt = jnp.minimum(pos + 1, w).astype(F32)
        pooled = win / cnt - ug
        mg = jnp.dot(pooled.astype(BF16), wg_ref[g], preferred_element_type=F32)
        mixed.append((mg * ps_ref[:, lo:hi]).astype(BF16))
    mix = jnp.concatenate([a_ref[...]] + mixed, axis=1)
    o_ref[...] = x_ref[...] + jnp.dot(mix, wo_ref[...], preferred_element_type=F32)


def _ab_out(a, u, x, wg, ps, wo):
    tm = ROW_TILE
    row = lambda i: (i, 0)
    return pl.pallas_call(
        _ab_out_kernel,
        grid=(N_ROWS // tm,),
        in_specs=[pl.BlockSpec((tm, A_WIDTH), row), pl.BlockSpec((tm, B_WIDTH), row),
                  pl.BlockSpec((tm, D_MODEL), row), _const_spec((len(POOL_WINDOWS), B_GROUP_DIM, B_GROUP_DIM)),
                  _const_spec((1, B_WIDTH)), _const_spec((A_WIDTH + B_WIDTH, D_MODEL))],
        out_specs=pl.BlockSpec((tm, D_MODEL), row),
        out_shape=jax.ShapeDtypeStruct((N_ROWS, D_MODEL), F32),
        scratch_shapes=[pltpu.VMEM((tm + POOL_HALO, B_WIDTH), F32), pltpu.VMEM((POOL_HALO, B_WIDTH), F32)],
        compiler_params=_params(("arbitrary",)),
        name="ab_out",
    )(a, u, x, wg, ps, wo)


def _c_in_kernel(x_ref, g_ref, w_ref, bd_ref, qg_ref, kg_ref, q_ref, k_ref, v_ref):
    xn = _rms_rows(x_ref[...], g_ref[...]).astype(BF16)
    bd = bd_ref[...]
    def finish(seg, cb):
        lo, hi = cb * NORM_SEG, (cb + 1) * NORM_SEG
        if cb < C_QW // NORM_SEG:
            q_ref[:, lo:hi] = _seg_norm(seg, bd, qg_ref[:, lo:hi]).astype(BF16)
        else:
            k_ref[...] = _seg_norm(seg, bd, kg_ref[...]).astype(BF16)

    pending = None
    for cb in range(C_QW // NORM_SEG + 1):
        seg = jnp.dot(xn, w_ref[:, cb * NORM_SEG:(cb + 1) * NORM_SEG], preferred_element_type=F32)
        if pending is not None:
            finish(*pending)
        pending = (seg, cb)
    v = jnp.dot(xn, w_ref[:, C_QW + NORM_SEG:], preferred_element_type=F32).astype(BF16)
    finish(*pending)
    ones = jnp.ones((v.shape[0], LANES), BF16)
    for g in range(C_KV_HEADS):
        v_ref[:, g * 2 * LANES:g * 2 * LANES + LANES] = v[:, g * LANES:(g + 1) * LANES]
        v_ref[:, g * 2 * LANES + LANES:(g + 1) * 2 * LANES] = ones


def _c_in(x, g, w, bd, qg, kg):
    tm = ROW_TILE
    row = lambda i: (i, 0)
    kvw = 2 * C_KV_HEADS * C_HEAD_DIM
    return pl.pallas_call(
        _c_in_kernel,
        grid=(N_ROWS // tm,),
        in_specs=[pl.BlockSpec((tm, D_MODEL), row), _const_spec((1, D_MODEL)),
                  _const_spec((D_MODEL, C_QW + 2 * kvw)), _const_spec((NORM_SEG, NORM_SEG)),
                  _const_spec((1, C_QW)), _const_spec((1, kvw))],
        out_specs=[pl.BlockSpec((tm, C_QW), row), pl.BlockSpec((tm, kvw), row), pl.BlockSpec((tm, 2 * kvw), row)],
        out_shape=[jax.ShapeDtypeStruct((N_ROWS, C_QW), BF16), jax.ShapeDtypeStruct((N_ROWS, kvw), BF16),
                   jax.ShapeDtypeStruct((N_ROWS, 2 * kvw), BF16)],
        compiler_params=_params(("arbitrary",)),
        name="c_in",
    )(x, g, w, bd, qg, kg)


def _swa_kernel(q_ref, kp_ref, kc_ref, vp_ref, vc_ref, bias_ref, sink_ref, o_ref):
    Q = C_QBLK
    i = pl.program_id(1)
    q = q_ref[...]
    lane = lax.broadcasted_iota(jnp.int32, (Q, LANES), 1)
    first_half = lane < C_HEAD_DIM
    zero = jnp.zeros((Q, LANES), BF16)
    rows = []
    for p in range(C_PAIRS):
        qp = q[:, p * LANES:(p + 1) * LANES]
        rows.append(jnp.where(first_half, qp, zero))
        rows.append(jnp.where(first_half, zero, qp))
    qs = jnp.concatenate(rows, axis=0)
    k2 = jnp.concatenate([kp_ref[...], kc_ref[...]], axis=0)
    v2 = jnp.concatenate([vp_ref[...], vc_ref[...]], axis=0)
    s = lax.dot_general(qs, k2, (((1,), (1,)), ((), ())), preferred_element_type=F32)
    s = s.reshape(C_GROUP, Q, 2 * Q) + bias_ref[...]
    col = lax.broadcasted_iota(jnp.int32, (C_GROUP, Q, 2 * Q), 2)
    s = jnp.where(jnp.logical_and(i == 0, col < Q), MASK_VALUE, s)
    sink = sink_ref[...][:, :, 0:1]
    m = jnp.maximum(jnp.max(s, axis=-1, keepdims=True), sink)
    p = jnp.exp(s - m)
    denom = jnp.sum(p, axis=-1, keepdims=True) + jnp.exp(sink - m)
    pn = (p / denom).reshape(C_GROUP * Q, 2 * Q).astype(BF16)
    o = jnp.dot(pn, v2, preferred_element_type=F32)
    for p_ in range(C_PAIRS):
        oa = o[(2 * p_) * Q:(2 * p_ + 1) * Q, :]
        ob = o[(2 * p_ + 1) * Q:(2 * p_ + 2) * Q, :]
        o_ref[:, p_ * LANES:(p_ + 1) * LANES] = jnp.where(first_half, oa, ob).astype(BF16)


def _swa_online(q, kdup, vaug, bias, sink):
    Q = C_QBLK
    nb = SEQ // Q
    gw = C_GROUP * C_HEAD_DIM
    cur = lambda b, i, g: (b * nb + i, g)
    prev = lambda b, i, g: (b * nb + jnp.maximum(i - 1, 0), g)
    vcur = lambda b, i, g: (b * nb + i, 2 * g)
    vprev = lambda b, i, g: (b * nb + jnp.maximum(i - 1, 0), 2 * g)
    return pl.pallas_call(
        _swa_kernel,
        grid=(BATCH, nb, C_KV_HEADS),
        in_specs=[pl.BlockSpec((Q, gw), cur),
                  pl.BlockSpec((Q, LANES), prev), pl.BlockSpec((Q, LANES), cur),
                  pl.BlockSpec((Q, LANES), vprev), pl.BlockSpec((Q, LANES), vcur),
                  pl.BlockSpec((C_GROUP, Q, 2 * Q), lambda b, i, g: (g, 0, 0)),
                  pl.BlockSpec((C_GROUP, 1, LANES), lambda b, i, g: (g, 0, 0))],
        out_specs=pl.BlockSpec((Q, gw), cur),
        out_shape=jax.ShapeDtypeStruct((N_ROWS, C_QW), BF16),
        compiler_params=_params(("arbitrary", "arbitrary", "arbitrary")),
        name="swa_online",
    )(q, kdup, kdup, vaug, vaug, bias, sink)


def _swa_bounded_kernel(sterm_ref, q_ref, kp_ref, kc_ref, vp_ref, vc_ref, tab_ref, o_ref):
    Q = C_QBLK
    first = jnp.where(pl.program_id(1) == 0, 1, 0)
    lane = lax.broadcasted_iota(jnp.int32, (Q, LANES), 1)
    first_half = lane < C_HEAD_DIM
    zero = jnp.zeros((Q, LANES), BF16)
    nt = (((1,), (1,)), ((), ()))
    for g in range(C_KV_HEADS):
        kcol = slice(g * LANES, (g + 1) * LANES)
        vcol = slice(g * 2 * LANES, (g + 1) * 2 * LANES)
        for sub in range(2):
            rows = []
            for p in range(C_PAIRS):
                c0 = g * C_GROUP * C_HEAD_DIM + p * LANES
                qp = q_ref[sub * Q:(sub + 1) * Q, c0:c0 + LANES]
                rows.append(jnp.where(first_half, qp, zero))
                rows.append(jnp.where(first_half, zero, qp))
            qs = jnp.concatenate(rows, axis=0)
            if sub == 0:
                k2 = jnp.concatenate([kp_ref[:, kcol], kc_ref[0:Q, kcol]], axis=0)
                v3 = jnp.concatenate([vp_ref[:, vcol], vc_ref[0:Q, vcol]], axis=0)
                tab = tab_ref[first, g * C_GROUP:(g + 1) * C_GROUP]
            else:
                k2 = kc_ref[:, kcol]
                v3 = vc_ref[:, vcol]
                tab = tab_ref[0, g * C_GROUP:(g + 1) * C_GROUP]
            s = lax.dot_general(qs, k2, nt, preferred_element_type=F32)
            p_ = jnp.exp(s.reshape(C_GROUP, Q, 2 * Q) + tab).astype(BF16).reshape(C_GROUP * Q, 2 * Q)
            o = jnp.dot(p_, v3, preferred_element_type=F32)
            for pr in range(C_PAIRS):
                outs = []
                for hh in (2 * pr, 2 * pr + 1):
                    blk = o[hh * Q:(hh + 1) * Q, :]
                    outs.append(blk[:, 0:LANES] / (blk[:, LANES:] + sterm_ref[g * C_GROUP + hh]))
                c0 = g * C_GROUP * C_HEAD_DIM + pr * LANES
                o_ref[sub * Q:(sub + 1) * Q, c0:c0 + LANES] = jnp.where(first_half, outs[0], outs[1]).astype(BF16)


def _swa_bounded(sterm, q, kdup, vaug, tab):
    Q = C_QBLK
    nb2 = SEQ // (2 * Q)
    kvw = 2 * C_KV_HEADS * C_HEAD_DIM
    cur = lambda b, i: (b * nb2 + i, 0)
    prev = lambda b, i: (b * 2 * nb2 + jnp.maximum(2 * i - 1, 0), 0)
    return pl.pallas_call(
        _swa_bounded_kernel,
        grid=(BATCH, nb2),
        in_specs=[pl.BlockSpec(memory_space=pltpu.SMEM),
                  pl.BlockSpec((2 * Q, C_QW), cur),
                  pl.BlockSpec((Q, kvw), prev), pl.BlockSpec((2 * Q, kvw), cur),
                  pl.BlockSpec((Q, 2 * kvw), prev), pl.BlockSpec((2 * Q, 2 * kvw), cur),
                  _const_spec((2, C_Q_HEADS, Q, 2 * Q))],
        out_specs=pl.BlockSpec((2 * Q, C_QW), cur),
        out_shape=jax.ShapeDtypeStruct((N_ROWS, C_QW), BF16),
        compiler_params=_params(("arbitrary", "arbitrary")),
        name="swa_bounded",
    )(sterm, q, kdup, kdup, vaug, vaug, tab)


def _c_out_kernel(o_ref, x_ref, w_ref, y_ref):
    y_ref[...] = x_ref[...] + jnp.dot(o_ref[...], w_ref[...], preferred_element_type=F32)


def _c_out(o, x, w):
    tm = ROW_TILE
    row = lambda i: (i, 0)
    return pl.pallas_call(
        _c_out_kernel,
        grid=(N_ROWS // tm,),
        in_specs=[pl.BlockSpec((tm, C_QW), row), pl.BlockSpec((tm, D_MODEL), row), _const_spec((C_QW, D_MODEL))],
        out_specs=pl.BlockSpec((tm, D_MODEL), row),
        out_shape=jax.ShapeDtypeStruct((N_ROWS, D_MODEL), F32),
        compiler_params=_params(("arbitrary",)),
        name="c_out",
    )(o, x, w)


def _ffn_kernel(x_ref, g_ref, wg_ref, wu_ref, cwg_ref, cwu_ref, cbg_ref, cbu_ref, wd_ref, o_ref,
                hs_ref, carry_ref, act_ref):
    tm = ROW_TILE
    H = CONV_HALO

    @pl.when(pl.program_id(0) % (SEQ // tm) == 0)
    def _():
        carry_ref[...] = jnp.zeros(carry_ref.shape, F32)

    x = x_ref[...]
    xn = _rms_rows(x, g_ref[...]).astype(BF16)

    def conv(c, part, w_ref, cw_ref, cb_ref):
        h = jnp.dot(xn, w_ref[c], preferred_element_type=F32)
        buf = hs_ref.at[c % 2, part]
        buf[0:H, :] = carry_ref[part, c]
        buf[H:, :] = h
        carry_ref[part, c] = h[tm - H:, :]
        cw = cw_ref[c]
        return (cb_ref[c] + cw[2:3, :] * h + cw[1:2, :] * buf[pl.ds(H - 1, tm), :]
                + cw[0:1, :] * buf[pl.ds(H - 2, tm), :])

    for c in range(FF_NCHUNK):
        gate = conv(c, 0, wg_ref, cwg_ref, cbg_ref)
        up = conv(c, 1, wu_ref, cwu_ref, cbu_ref)
        act = gate / (1.0 + jnp.exp(-gate)) * up
        act_ref[:, c * FF_CHUNK:(c + 1) * FF_CHUNK] = act.astype(BF16)
    o_ref[...] = x + jnp.dot(act_ref[...], wd_ref[...], preferred_element_type=F32)


def _ffn(x, g, wg, wu, cwg, cwu, cbg, cbu, wd):
    tm = ROW_TILE
    row = lambda i: (i, 0)
    return pl.pallas_call(
        _ffn_kernel,
        grid=(N_ROWS // tm,),
        in_specs=[pl.BlockSpec((tm, D_MODEL), row), _const_spec((1, D_MODEL)),
                  _const_spec((FF_NCHUNK, D_MODEL, FF_CHUNK)), _const_spec((FF_NCHUNK, D_MODEL, FF_CHUNK)),
                  _const_spec((FF_NCHUNK, CONV_W, FF_CHUNK)), _const_spec((FF_NCHUNK, CONV_W, FF_CHUNK)),
                  _const_spec((FF_NCHUNK, 1, FF_CHUNK)), _const_spec((FF_NCHUNK, 1, FF_CHUNK)),
                  _const_spec((D_FF, D_MODEL))],
        out_specs=pl.BlockSpec((tm, D_MODEL), row),
        out_shape=jax.ShapeDtypeStruct((N_ROWS, D_MODEL), F32),
        scratch_shapes=[pltpu.VMEM((2, 2, tm + CONV_HALO, FF_CHUNK), F32),
                        pltpu.VMEM((2, FF_NCHUNK, CONV_HALO, FF_CHUNK), F32),
                        pltpu.VMEM((tm, D_FF), BF16)],
        compiler_params=_params(("arbitrary",)),
        name="ffn",
    )(x, g, wg, wu, cwg, cwu, cbg, cbu, wd)


def _chunk_cols(w):
    r = w.shape[0]
    return w.reshape(r, FF_NCHUNK, FF_CHUNK).transpose(1, 0, 2)


def _alibi_slopes(n):
    return 2.0 ** (-8.0 * jnp.arange(1, n + 1, dtype=F32) / n)


def _block_diag_ones():
    seg = jnp.arange(NORM_SEG) // 64
    return (seg[:, None] == seg[None, :]).astype(BF16)


def _row(v):
    return v.reshape(1, -1).astype(F32)


def kernel(x, ab_norm, ab_w_in, a_q_norm, a_k_norm, a_lambda, a_sub_norm, b_w_group, b_scale, ab_w_out,
           c_norm, c_w_in, c_q_norm, c_k_norm, c_sinks, c_w_out, f_norm, f_w_up, f_conv, f_conv_b, f_w_down):
    bd = _block_diag_ones()
    h = x.reshape(N_ROWS, D_MODEL)

    a_slopes = _alibi_slopes(A_HEADS)
    ti = jnp.arange(ATT_TILE, dtype=F32)
    a_dtab = -a_slopes[:, None, None] * (ti[:, None] - ti[None, :])[None]
    c_slopes = _alibi_slopes(C_Q_HEADS)
    rel = C_QBLK + jnp.arange(C_QBLK)[:, None] - jnp.arange(2 * C_QBLK)[None, :]
    c_valid = (rel >= 0) & (rel < C_WINDOW)
    c_bias = jnp.where(c_valid[None], -c_slopes[:, None, None] * rel.astype(F32)[None], MASK_VALUE)

    for layer in range(DEPTH):
        if layer % 2 == 0:
            e = layer // 2
            lam_init = 0.8 - 0.6 * math.exp(-0.3 * layer)
            lp = a_lambda[e].astype(F32)
            lam = jnp.exp(jnp.sum(lp[0] * lp[1])) - jnp.exp(jnp.sum(lp[2] * lp[3])) + lam_init
            qg = _row(jnp.tile(a_q_norm[e], 2 * A_HEADS)) * (A_HEAD_DIM ** -0.5)
            kg = _row(jnp.tile(a_k_norm[e], 2 * A_HEADS))
            q, kaug, vaug, u = _ab_in(h, _row(ab_norm[e]), ab_w_in[e].astype(BF16), bd, qg, kg)
            subg = _row(a_sub_norm[e]) * (1.0 - lam_init)
            lam1 = lam.reshape(1)
            bound = (1.02 * A_HEAD_DIM ** 0.5) * jnp.max(jnp.abs(a_q_norm[e])) * jnp.max(jnp.abs(a_k_norm[e]))
            bound = bound.astype(F32).reshape(1)
            a_out = lax.cond(
                bound[0] <= SCORE_BOUND_LIMIT,
                lambda: _diff_attn_bounded(a_slopes, lam1, bound, q, kaug, vaug, subg),
                lambda: _diff_attn_online(a_slopes, lam1, q, kaug, vaug, a_dtab, subg))
            h = _ab_out(a_out, u, h, b_w_group[e].astype(BF16), _row(b_scale[e]), ab_w_out[e].astype(BF16))
        else:
            o = layer // 2
            w = c_w_in[o]
            hd = C_HEAD_DIM
            wk = w[:, C_QW:C_QW + C_KV_HEADS * hd]
            wv = w[:, C_QW + C_KV_HEADS * hd:]
            dup = lambda m: jnp.concatenate([m[:, :hd], m[:, :hd], m[:, hd:], m[:, hd:]], axis=1)
            w_cat = jnp.concatenate([w[:, :C_QW], dup(wk), dup(wv)], axis=1).astype(BF16)
            qg = _row(jnp.tile(c_q_norm[o], C_Q_HEADS)) * (C_HEAD_DIM ** -0.5)
            kg = _row(jnp.tile(c_k_norm[o], 2 * C_KV_HEADS))
            q, kdup, vaug = _c_in(h, _row(c_norm[o]), w_cat, bd, qg, kg)
            sinks = c_sinks[o].astype(F32)
            bound = (1.02 * C_HEAD_DIM ** 0.5) * jnp.max(jnp.abs(c_q_norm[o])) * jnp.max(jnp.abs(c_k_norm[o]))
            bound = bound.astype(F32)
            shift = jnp.maximum(bound, sinks)
            tab = c_bias - shift[:, None, None]
            tab = jnp.stack([tab, jnp.where(jnp.arange(2 * C_QBLK) < C_QBLK, MASK_VALUE, tab)])
            sterm = jnp.exp(sinks - shift)
            sink_b = jnp.broadcast_to(sinks[:, None, None], (C_Q_HEADS, 1, LANES))
            att = lax.cond(
                bound <= SCORE_BOUND_LIMIT,
                lambda: _swa_bounded(sterm, q, kdup, vaug, tab),
                lambda: _swa_online(q, kdup, vaug, c_bias, sink_b))
            h = _c_out(att, h, c_w_out[o].astype(BF16))
        wup = f_w_up[layer].astype(BF16)
        cw = f_conv[layer].astype(F32)
        cb = f_conv_b[layer].astype(F32).reshape(1, -1)
        h = _ffn(h, _row(f_norm[layer]),
                 _chunk_cols(wup[:, :D_FF]), _chunk_cols(wup[:, D_FF:]),
                 _chunk_cols(cw[:, :D_FF]), _chunk_cols(cw[:, D_FF:]),
                 _chunk_cols(cb[:, :D_FF]), _chunk_cols(cb[:, D_FF:]),
                 f_w_down[layer].astype(BF16))
    return h.reshape(BATCH, SEQ, D_MODEL)
```

```python
import functools
import math

import jax
import jax.numpy as jnp
from jax import lax
from jax.experimental import pallas as pl
from jax.experimental.pallas import tpu as pltpu

D_MODEL = 1024
BATCH = 8
SEQ = 4096
DEPTH = 4
EPS = 1e-6
MASK_VALUE = -1e30

A_WIDTH = 512
A_HEAD_DIM = 64
A_HEADS = 4
A_VDIM = 2 * A_HEAD_DIM
B_WIDTH = 512
POOL_WINDOWS = (2, 4, 8, 16)
B_GROUP_DIM = 128
POOL_HALO = 16

C_HEAD_DIM = 64
C_Q_HEADS = 16
C_KV_HEADS = 2
C_GROUP = 8
C_WINDOW = 128
C_QBLK = 128
C_QW = C_Q_HEADS * C_HEAD_DIM
C_PAIRS = C_GROUP // 2

D_FF = 2816
CONV_W = 3
FF_CHUNK = 256
FF_NCHUNK = D_FF // FF_CHUNK
CONV_HALO = 8

LANES = 128
NORM_SEG = 256

ROW_TILE = 512
ATT_TILE = 256
ATT_QTILE = 2 * ATT_TILE
ATT_KCHUNK = 512
SCORE_BOUND_LIMIT = 30.0
VMEM_LIMIT = 56 * 1024 * 1024

N_ROWS = BATCH * SEQ
F32 = jnp.float32
BF16 = jnp.bfloat16


def _params(sem, vmem=VMEM_LIMIT):
    return pltpu.CompilerParams(dimension_semantics=sem, vmem_limit_bytes=vmem)


def _const_spec(shape):
    nd = len(shape)
    return pl.BlockSpec(shape, lambda *_: (0,) * nd, pipeline_mode=pl.Buffered(1))


def _rms_rows(x, g):
    ms = jnp.mean(x * x, axis=-1, keepdims=True)
    return x * lax.rsqrt(ms + EPS) * g


def _seg_norm(seg, bd, gain):
    ss = jnp.dot((seg * seg).astype(BF16), bd, preferred_element_type=F32)
    return seg * lax.rsqrt(ss * (1.0 / 64.0) + EPS) * gain


def _ab_in_kernel(x_ref, g_ref, w_ref, bd_ref, qg_ref, kg_ref, q_ref, k_ref, v_ref, u_ref):
    tm = ROW_TILE
    xn = _rms_rows(x_ref[...], g_ref[...]).astype(BF16)
    bd = bd_ref[...]
    lane = lax.broadcasted_iota(jnp.int32, (tm, LANES), 1)
    pos = (pl.program_id(0) % (SEQ // tm)) * tm + lax.broadcasted_iota(jnp.int32, (tm, LANES), 0)
    kaug = jnp.where(lane == 0, pos >> 6, jnp.where(lane == 1, pos & 63, jnp.where(lane == 2, 1, 0)))
    kaug = kaug.astype(F32).astype(BF16)
    def finish(seg, is_q, cb):
        lo, hi = cb * NORM_SEG, (cb + 1) * NORM_SEG
        if is_q:
            q_ref[:, lo:hi] = _seg_norm(seg, bd, qg_ref[:, lo:hi]).astype(BF16)
            return
        kn = _seg_norm(seg, bd, kg_ref[:, lo:hi]).astype(BF16)
        for hh in range(NORM_SEG // A_VDIM):
            h = cb * (NORM_SEG // A_VDIM) + hh
            k_ref[:, h * 2 * A_VDIM:h * 2 * A_VDIM + A_VDIM] = kn[:, hh * A_VDIM:(hh + 1) * A_VDIM]
            k_ref[:, h * 2 * A_VDIM + A_VDIM:(h + 1) * 2 * A_VDIM] = kaug

    pending = None
    for cb in range(A_WIDTH // NORM_SEG):
        for is_q in (True, False):
            c0 = cb * NORM_SEG + (0 if is_q else A_WIDTH)
            seg = jnp.dot(xn, w_ref[:, c0:c0 + NORM_SEG], preferred_element_type=F32)
            if pending is not None:
                finish(*pending)
            pending = (seg, is_q, cb)
    v = jnp.dot(xn, w_ref[:, 2 * A_WIDTH:3 * A_WIDTH], preferred_element_type=F32).astype(BF16)
    finish(*pending)
    ones = jnp.ones((v.shape[0], A_VDIM), BF16)
    for h in range(A_HEADS):
        v_ref[:, h * 2 * A_VDIM:h * 2 * A_VDIM + A_VDIM] = v[:, h * A_VDIM:(h + 1) * A_VDIM]
        v_ref[:, h * 2 * A_VDIM + A_VDIM:(h + 1) * 2 * A_VDIM] = ones
    u_ref[...] = jnp.dot(xn, w_ref[:, 3 * A_WIDTH:], preferred_element_type=F32)


def _ab_in(x, g, w, bd, qg, kg):
    tm = ROW_TILE
    row = lambda i: (i, 0)
    return pl.pallas_call(
        _ab_in_kernel,
        grid=(N_ROWS // tm,),
        in_specs=[pl.BlockSpec((tm, D_MODEL), row), _const_spec((1, D_MODEL)),
                  _const_spec((D_MODEL, 3 * A_WIDTH + B_WIDTH)), _const_spec((NORM_SEG, NORM_SEG)),
                  _const_spec((1, A_WIDTH)), _const_spec((1, A_WIDTH))],
        out_specs=[pl.BlockSpec((tm, A_WIDTH), row), pl.BlockSpec((tm, 2 * A_WIDTH), row),
                   pl.BlockSpec((tm, 2 * A_WIDTH), row), pl.BlockSpec((tm, B_WIDTH), row)],
        out_shape=[jax.ShapeDtypeStruct((N_ROWS, A_WIDTH), BF16), jax.ShapeDtypeStruct((N_ROWS, 2 * A_WIDTH), BF16),
                   jax.ShapeDtypeStruct((N_ROWS, 2 * A_WIDTH), BF16), jax.ShapeDtypeStruct((N_ROWS, B_WIDTH), F32)],
        compiler_params=_params(("arbitrary",)),
        name="ab_in",
    )(x, g, w, bd, qg, kg)


def _diff_attn_kernel(slope_ref, lam_ref, q_ref, k_ref, v_ref, d_ref, sg_ref, o_ref, qs_ref, m_ref, acc_ref):
    T = ATT_TILE
    h = pl.program_id(1)
    i = pl.program_id(2)
    q = q_ref[...]
    lane = lax.broadcasted_iota(jnp.int32, q.shape, 1)
    zero = jnp.zeros_like(q)
    qs_ref[0:T, :] = jnp.where(lane < A_HEAD_DIM, q, zero)
    qs_ref[T:2 * T, :] = jnp.where(lane >= A_HEAD_DIM, q, zero)
    m_ref[...] = jnp.full(m_ref.shape, MASK_VALUE, F32)
    acc_ref[...] = jnp.zeros(acc_ref.shape, F32)
    slope = slope_ref[h]
    dtab = d_ref[...]

    def step(j, masked):
        koff = pl.multiple_of(j * T, T)
        kt = k_ref[pl.ds(koff, T), :]
        vt = v_ref[pl.ds(koff, T), :]
        s = lax.dot_general(qs_ref[...], kt, (((1,), (1,)), ((), ())), preferred_element_type=F32)
        bias = dtab
        if masked:
            r = lax.broadcasted_iota(jnp.int32, (T, T), 0)
            c = lax.broadcasted_iota(jnp.int32, (T, T), 1)
            bias = jnp.where(c > r, MASK_VALUE, dtab)
        s = s + jnp.concatenate([bias, bias], axis=0)
        off = slope * ((j - i) * T).astype(F32)
        m_prev = m_ref[...]
        m_new = jnp.maximum(m_prev, jnp.max(s, axis=1, keepdims=True) + off)
        alpha = jnp.exp(m_prev - m_new)
        shift = m_new - off
        p = jnp.exp(s - jnp.concatenate([shift] * (T // LANES), axis=1))
        pv = jnp.dot(p.astype(BF16), vt, preferred_element_type=F32)
        acc_ref[...] = acc_ref[...] * jnp.concatenate([alpha, alpha], axis=1) + pv
        m_ref[...] = m_new

    def body(j, carry):
        step(j, False)
        return carry

    lax.fori_loop(0, i, body, 0)
    step(i, True)

    acc = acc_ref[...]
    o0 = acc[0:T, 0:A_VDIM] / acc[0:T, A_VDIM:]
    o1 = acc[T:, 0:A_VDIM] / acc[T:, A_VDIM:]
    o = o0 - lam_ref[0] * o1
    o_ref[...] = _rms_rows(o, sg_ref[...]).astype(BF16)


def _diff_attn_online(slopes, lam, q, kaug, vaug, dtab, subg):
    T = ATT_TILE
    nq = SEQ // T
    smem = pl.BlockSpec(memory_space=pltpu.SMEM)
    return pl.pallas_call(
        _diff_attn_kernel,
        grid=(BATCH, A_HEADS, nq),
        in_specs=[smem, smem,
                  pl.BlockSpec((T, A_VDIM), lambda b, h, i: (b * nq + i, h)),
                  pl.BlockSpec((SEQ, A_VDIM), lambda b, h, i: (b, 2 * h)),
                  pl.BlockSpec((SEQ, 2 * A_VDIM), lambda b, h, i: (b, h)),
                  pl.BlockSpec((None, T, T), lambda b, h, i: (h, 0, 0)),
                  pl.BlockSpec((1, A_VDIM), lambda b, h, i: (0, 0))],
        out_specs=pl.BlockSpec((T, A_VDIM), lambda b, h, i: (b * nq + i, h)),
        out_shape=jax.ShapeDtypeStruct((N_ROWS, A_WIDTH), BF16),
        scratch_shapes=[pltpu.VMEM((2 * T, A_VDIM), BF16), pltpu.VMEM((2 * T, LANES), F32),
                        pltpu.VMEM((2 * T, 2 * A_VDIM), F32)],
        compiler_params=_params(("arbitrary", "arbitrary", "arbitrary")),
        name="diff_attn_online",
    )(slopes, lam, q, kaug, vaug, dtab, subg)


def _diff_attn_bounded_kernel(slope_ref, lam_ref, bound_ref, q_ref, k_ref, v_ref, sg_ref, o_ref):
    TQ, HB, C = ATT_QTILE, ATT_TILE, ATT_KCHUNK
    slope = slope_ref[pl.program_id(1)]
    lam = lam_ref[0]
    lane = lax.broadcasted_iota(jnp.int32, (TQ, A_VDIM), 1)
    row = lax.broadcasted_iota(jnp.int32, (TQ, A_VDIM), 0)
    zero = jnp.zeros((TQ, A_VDIM), BF16)
    r = lax.broadcasted_iota(jnp.int32, (HB, HB), 0)
    c = lax.broadcasted_iota(jnp.int32, (HB, HB), 1)
    tri = jnp.where(c > r, MASK_VALUE, 0.0).astype(F32)
    tri2 = jnp.concatenate([tri, tri], axis=0)
    hidden = jnp.full((2 * HB, HB), MASK_VALUE, F32)
    visible = jnp.zeros((2 * HB, HB), F32)
    mask = jnp.concatenate([jnp.concatenate([tri2, hidden], axis=1),
                            jnp.concatenate([visible, tri2], axis=1)], axis=0)
    nt = (((1,), (1,)), ((), ()))

    for i in range(SEQ // TQ):
        q = q_ref[i * TQ:(i + 1) * TQ, :]
        t = (i * TQ + row).astype(F32)
        row_shift = -(slope * t + bound_ref[0])
        aug = jnp.where(lane == 0, 64.0 * slope, jnp.where(lane == 1, slope, jnp.where(lane == 2, row_shift, 0.0)))
        aug = aug.astype(BF16)
        maps = (jnp.where(lane < A_HEAD_DIM, q, zero), jnp.where(lane >= A_HEAD_DIM, q, zero))
        qs = jnp.concatenate(
            [jnp.concatenate([maps[mp][blk * HB:(blk + 1) * HB, :], aug[blk * HB:(blk + 1) * HB, :]], axis=1)
             for blk in range(TQ // HB) for mp in range(2)], axis=0)
        acc = None
        for j in range(i + 1):
            s = lax.dot_general(qs, k_ref[j * C:(j + 1) * C, :], nt, preferred_element_type=F32)
            if j == i:
                s = s + mask
            pv = jnp.dot(jnp.exp(s).astype(BF16), v_ref[j * C:(j + 1) * C, :], preferred_element_type=F32)
            acc = pv if acc is None else acc + pv
        for blk in range(TQ // HB):
            a0 = acc[(2 * blk) * HB:(2 * blk + 1) * HB, :]
            a1 = acc[(2 * blk + 1) * HB:(2 * blk + 2) * HB, :]
            o = a0[:, 0:A_VDIM] / a0[:, A_VDIM:] - lam * (a1[:, 0:A_VDIM] / a1[:, A_VDIM:])
            o_ref[i * TQ + blk * HB:i * TQ + (blk + 1) * HB, :] = _rms_rows(o, sg_ref[...]).astype(BF16)


def _diff_attn_bounded(slopes, lam, bound, q, kaug, vaug, subg):
    smem = pl.BlockSpec(memory_space=pltpu.SMEM)
    bh = lambda b, h: (b, h)
    return pl.pallas_call(
        _diff_attn_bounded_kernel,
        grid=(BATCH, A_HEADS),
        in_specs=[smem, smem, smem,
                  pl.BlockSpec((SEQ, A_VDIM), bh), pl.BlockSpec((SEQ, 2 * A_VDIM), bh),
                  pl.BlockSpec((SEQ, 2 * A_VDIM), bh), pl.BlockSpec((1, A_VDIM), lambda b, h: (0, 0))],
        out_specs=pl.BlockSpec((SEQ, A_VDIM), bh),
        out_shape=jax.ShapeDtypeStruct((N_ROWS, A_WIDTH), BF16),
        compiler_params=_params(("arbitrary", "arbitrary")),
        name="diff_attn_bounded",
    )(slopes, lam, bound, q, kaug, vaug, subg)


def _ab_mix(a_ref, u_ref, x_ref, wg_ref, ps_ref, wo_ref, uext_ref, carry_ref):
    tm = ROW_TILE
    tiles_per_seq = SEQ // tm
    it = pl.program_id(0) % tiles_per_seq

    @pl.when(it == 0)
    def _():
        carry_ref[...] = jnp.zeros(carry_ref.shape, F32)

    u = u_ref[...]
    uext_ref[0:POOL_HALO, :] = carry_ref[...]
    uext_ref[POOL_HALO:, :] = u
    carry_ref[...] = u[tm - POOL_HALO:, :]
    pos = it * tm + lax.broadcasted_iota(jnp.int32, (tm, 1), 0)
    mixed = []
    for g, w in enumerate(POOL_WINDOWS):
        lo, hi = g * B_GROUP_DIM, (g + 1) * B_GROUP_DIM
        ug = u[:, lo:hi]
        win = ug
        for d in range(1, w):
            win = win + uext_ref[pl.ds(POOL_HALO - d, tm), lo:hi]
        cnt = jnp.minimum(pos + 1, w).astype(F32)
        pooled = win / cnt - ug
        mg = jnp.dot(pooled.astype(BF16), wg_ref[g], preferred_element_type=F32)
        mixed.append((mg * ps_ref[:, lo:hi]).astype(BF16))
    mix = jnp.concatenate([a_ref[...]] + mixed, axis=1)
    return x_ref[...] + jnp.dot(mix, wo_ref[...], preferred_element_type=F32)


def _c_in_kernel(x_ref, g_ref, w_ref, bd_ref, qg_ref, kg_ref, q_ref, k_ref, v_ref):
    xn = _rms_rows(x_ref[...], g_ref[...]).astype(BF16)
    bd = bd_ref[...]
    def finish(seg, cb):
        lo, hi = cb * NORM_SEG, (cb + 1) * NORM_SEG
        if cb < C_QW // NORM_SEG:
            q_ref[:, lo:hi] = _seg_norm(seg, bd, qg_ref[:, lo:hi]).astype(BF16)
        else:
            k_ref[...] = _seg_norm(seg, bd, kg_ref[...]).astype(BF16)

    pending = None
    for cb in range(C_QW // NORM_SEG + 1):
        seg = jnp.dot(xn, w_ref[:, cb * NORM_SEG:(cb + 1) * NORM_SEG], preferred_element_type=F32)
        if pending is not None:
            finish(*pending)
        pending = (seg, cb)
    v = jnp.dot(xn, w_ref[:, C_QW + NORM_SEG:], preferred_element_type=F32).astype(BF16)
    finish(*pending)
    ones = jnp.ones((v.shape[0], LANES), BF16)
    for g in range(C_KV_HEADS):
        v_ref[:, g * 2 * LANES:g * 2 * LANES + LANES] = v[:, g * LANES:(g + 1) * LANES]
        v_ref[:, g * 2 * LANES + LANES:(g + 1) * 2 * LANES] = ones


def _c_in(x, g, w, bd, qg, kg):
    tm = ROW_TILE
    row = lambda i: (i, 0)
    kvw = 2 * C_KV_HEADS * C_HEAD_DIM
    return pl.pallas_call(
        _c_in_kernel,
        grid=(N_ROWS // tm,),
        in_specs=[pl.BlockSpec((tm, D_MODEL), row), _const_spec((1, D_MODEL)),
                  _const_spec((D_MODEL, C_QW + 2 * kvw)), _const_spec((NORM_SEG, NORM_SEG)),
                  _const_spec((1, C_QW)), _const_spec((1, kvw))],
        out_specs=[pl.BlockSpec((tm, C_QW), row), pl.BlockSpec((tm, kvw), row), pl.BlockSpec((tm, 2 * kvw), row)],
        out_shape=[jax.ShapeDtypeStruct((N_ROWS, C_QW), BF16), jax.ShapeDtypeStruct((N_ROWS, kvw), BF16),
                   jax.ShapeDtypeStruct((N_ROWS, 2 * kvw), BF16)],
        compiler_params=_params(("arbitrary",)),
        name="c_in",
    )(x, g, w, bd, qg, kg)


def _swa_kernel(q_ref, kp_ref, kc_ref, vp_ref, vc_ref, bias_ref, sink_ref, o_ref):
    Q = C_QBLK
    i = pl.program_id(1)
    q = q_ref[...]
    lane = lax.broadcasted_iota(jnp.int32, (Q, LANES), 1)
    first_half = lane < C_HEAD_DIM
    zero = jnp.zeros((Q, LANES), BF16)
    rows = []
    for p in range(C_PAIRS):
        qp = q[:, p * LANES:(p + 1) * LANES]
        rows.append(jnp.where(first_half, qp, zero))
        rows.append(jnp.where(first_half, zero, qp))
    qs = jnp.concatenate(rows, axis=0)
    k2 = jnp.concatenate([kp_ref[...], kc_ref[...]], axis=0)
    v2 = jnp.concatenate([vp_ref[...], vc_ref[...]], axis=0)
    s = lax.dot_general(qs, k2, (((1,), (1,)), ((), ())), preferred_element_type=F32)
    s = s.reshape(C_GROUP, Q, 2 * Q) + bias_ref[...]
    col = lax.broadcasted_iota(jnp.int32, (C_GROUP, Q, 2 * Q), 2)
    s = jnp.where(jnp.logical_and(i == 0, col < Q), MASK_VALUE, s)
    sink = sink_ref[...][:, :, 0:1]
    m = jnp.maximum(jnp.max(s, axis=-1, keepdims=True), sink)
    p = jnp.exp(s - m)
    denom = jnp.sum(p, axis=-1, keepdims=True) + jnp.exp(sink - m)
    pn = (p / denom).reshape(C_GROUP * Q, 2 * Q).astype(BF16)
    o = jnp.dot(pn, v2, preferred_element_type=F32)
    for p_ in range(C_PAIRS):
        oa = o[(2 * p_) * Q:(2 * p_ + 1) * Q, :]
        ob = o[(2 * p_ + 1) * Q:(2 * p_ + 2) * Q, :]
        o_ref[:, p_ * LANES:(p_ + 1) * LANES] = jnp.where(first_half, oa, ob).astype(BF16)


def _swa_online(q, kdup, vaug, bias, sink):
    Q = C_QBLK
    nb = SEQ // Q
    gw = C_GROUP * C_HEAD_DIM
    cur = lambda b, i, g: (b * nb + i, g)
    prev = lambda b, i, g: (b * nb + jnp.maximum(i - 1, 0), g)
    vcur = lambda b, i, g: (b * nb + i, 2 * g)
    vprev = lambda b, i, g: (b * nb + jnp.maximum(i - 1, 0), 2 * g)
    return pl.pallas_call(
        _swa_kernel,
        grid=(BATCH, nb, C_KV_HEADS),
        in_specs=[pl.BlockSpec((Q, gw), cur),
                  pl.BlockSpec((Q, LANES), prev), pl.BlockSpec((Q, LANES), cur),
                  pl.BlockSpec((Q, LANES), vprev), pl.BlockSpec((Q, LANES), vcur),
                  pl.BlockSpec((C_GROUP, Q, 2 * Q), lambda b, i, g: (g, 0, 0)),
                  pl.BlockSpec((C_GROUP, 1, LANES), lambda b, i, g: (g, 0, 0))],
        out_specs=pl.BlockSpec((Q, gw), cur),
        out_shape=jax.ShapeDtypeStruct((N_ROWS, C_QW), BF16),
        compiler_params=_params(("arbitrary", "arbitrary", "arbitrary")),
        name="swa_online",
    )(q, kdup, kdup, vaug, vaug, bias, sink)


def _swa_bounded_kernel(sterm_ref, q_ref, kp_ref, kc_ref, vp_ref, vc_ref, tab_ref, o_ref):
    Q = C_QBLK
    first = jnp.where(pl.program_id(1) == 0, 1, 0)
    lane = lax.broadcasted_iota(jnp.int32, (Q, LANES), 1)
    first_half = lane < C_HEAD_DIM
    zero = jnp.zeros((Q, LANES), BF16)
    nt = (((1,), (1,)), ((), ()))
    for g in range(C_KV_HEADS):
        kcol = slice(g * LANES, (g + 1) * LANES)
        vcol = slice(g * 2 * LANES, (g + 1) * 2 * LANES)
        for sub in range(2):
            rows = []
            for p in range(C_PAIRS):
                c0 = g * C_GROUP * C_HEAD_DIM + p * LANES
                qp = q_ref[sub * Q:(sub + 1) * Q, c0:c0 + LANES]
                rows.append(jnp.where(first_half, qp, zero))
                rows.append(jnp.where(first_half, zero, qp))
            qs = jnp.concatenate(rows, axis=0)
            if sub == 0:
                k2 = jnp.concatenate([kp_ref[:, kcol], kc_ref[0:Q, kcol]], axis=0)
                v3 = jnp.concatenate([vp_ref[:, vcol], vc_ref[0:Q, vcol]], axis=0)
                tab = tab_ref[first, g * C_GROUP:(g + 1) * C_GROUP]
            else:
                k2 = kc_ref[:, kcol]
                v3 = vc_ref[:, vcol]
                tab = tab_ref[0, g * C_GROUP:(g + 1) * C_GROUP]
            s = lax.dot_general(qs, k2, nt, preferred_element_type=F32)
            p_ = jnp.exp(s.reshape(C_GROUP, Q, 2 * Q) + tab).astype(BF16).reshape(C_GROUP * Q, 2 * Q)
            o = jnp.dot(p_, v3, preferred_element_type=F32)
            for pr in range(C_PAIRS):
                outs = []
                for hh in (2 * pr, 2 * pr + 1):
                    blk = o[hh * Q:(hh + 1) * Q, :]
                    outs.append(blk[:, 0:LANES] / (blk[:, LANES:] + sterm_ref[g * C_GROUP + hh]))
                c0 = g * C_GROUP * C_HEAD_DIM + pr * LANES
                o_ref[sub * Q:(sub + 1) * Q, c0:c0 + LANES] = jnp.where(first_half, outs[0], outs[1]).astype(BF16)


def _swa_bounded(sterm, q, kdup, vaug, tab):
    Q = C_QBLK
    nb2 = SEQ // (2 * Q)
    kvw = 2 * C_KV_HEADS * C_HEAD_DIM
    cur = lambda b, i: (b * nb2 + i, 0)
    prev = lambda b, i: (b * 2 * nb2 + jnp.maximum(2 * i - 1, 0), 0)
    return pl.pallas_call(
        _swa_bounded_kernel,
        grid=(BATCH, nb2),
        in_specs=[pl.BlockSpec(memory_space=pltpu.SMEM),
                  pl.BlockSpec((2 * Q, C_QW), cur),
                  pl.BlockSpec((Q, kvw), prev), pl.BlockSpec((2 * Q, kvw), cur),
                  pl.BlockSpec((Q, 2 * kvw), prev), pl.BlockSpec((2 * Q, 2 * kvw), cur),
                  _const_spec((2, C_Q_HEADS, Q, 2 * Q))],
        out_specs=pl.BlockSpec((2 * Q, C_QW), cur),
        out_shape=jax.ShapeDtypeStruct((N_ROWS, C_QW), BF16),
        compiler_params=_params(("arbitrary", "arbitrary")),
        name="swa_bounded",
    )(sterm, q, kdup, kdup, vaug, vaug, tab)


def _ffn_tail(x, g_ref, wup_ref, cw_ref, cb_ref, wd_ref, o_ref, hs_ref, carry_ref, act_ref):
    tm = ROW_TILE
    H = CONV_HALO

    @pl.when(pl.program_id(0) % (SEQ // tm) == 0)
    def _():
        carry_ref[...] = jnp.zeros(carry_ref.shape, F32)

    xn = _rms_rows(x, g_ref[...]).astype(BF16)

    def conv(c, part):
        cols = slice(part * D_FF + c * FF_CHUNK, part * D_FF + (c + 1) * FF_CHUNK)
        h = jnp.dot(xn, wup_ref[:, cols], preferred_element_type=F32)
        buf = hs_ref.at[c % 2, part]
        buf[0:H, :] = carry_ref[part, c]
        buf[H:, :] = h
        carry_ref[part, c] = h[tm - H:, :]
        return (cb_ref[:, cols] + cw_ref[2:3, cols] * h + cw_ref[1:2, cols] * buf[pl.ds(H - 1, tm), :]
                + cw_ref[0:1, cols] * buf[pl.ds(H - 2, tm), :])

    for c in range(FF_NCHUNK):
        gate = conv(c, 0)
        up = conv(c, 1)
        act = gate / (1.0 + jnp.exp(-gate)) * up
        act_ref[:, c * FF_CHUNK:(c + 1) * FF_CHUNK] = act.astype(BF16)
    o_ref[...] = x + jnp.dot(act_ref[...], wd_ref[...], preferred_element_type=F32)


def _ab_ffn_kernel(a_ref, u_ref, x_ref, wg_ref, ps_ref, wo_ref, g_ref, wup_ref, cw_ref, cb_ref, wd_ref, o_ref,
                   hs_ref, carry_ref, act_ref, uext_ref, ucarry_ref):
    x = _ab_mix(a_ref, u_ref, x_ref, wg_ref, ps_ref, wo_ref, uext_ref, ucarry_ref)
    _ffn_tail(x, g_ref, wup_ref, cw_ref, cb_ref, wd_ref, o_ref, hs_ref, carry_ref, act_ref)


def _c_ffn_kernel(att_ref, x_ref, wo_ref, g_ref, wup_ref, cw_ref, cb_ref, wd_ref, o_ref, hs_ref, carry_ref, act_ref):
    x = x_ref[...] + jnp.dot(att_ref[...], wo_ref[...], preferred_element_type=F32)
    _ffn_tail(x, g_ref, wup_ref, cw_ref, cb_ref, wd_ref, o_ref, hs_ref, carry_ref, act_ref)


def _ffn_specs():
    tm = ROW_TILE
    in_specs = [_const_spec((1, D_MODEL)), _const_spec((D_MODEL, 2 * D_FF)), _const_spec((CONV_W, 2 * D_FF)),
                _const_spec((1, 2 * D_FF)), _const_spec((D_FF, D_MODEL))]
    scratch = [pltpu.VMEM((2, 2, tm + CONV_HALO, FF_CHUNK), F32),
               pltpu.VMEM((2, FF_NCHUNK, CONV_HALO, FF_CHUNK), F32),
               pltpu.VMEM((tm, D_FF), BF16)]
    return in_specs, scratch


def _ab_ffn(a, u, x, wg, ps, wo, ffn_args):
    tm = ROW_TILE
    row = lambda i: (i, 0)
    f_in, f_scratch = _ffn_specs()
    return pl.pallas_call(
        _ab_ffn_kernel,
        grid=(N_ROWS // tm,),
        in_specs=[pl.BlockSpec((tm, A_WIDTH), row), pl.BlockSpec((tm, B_WIDTH), row),
                  pl.BlockSpec((tm, D_MODEL), row), _const_spec((len(POOL_WINDOWS), B_GROUP_DIM, B_GROUP_DIM)),
                  _const_spec((1, B_WIDTH)), _const_spec((A_WIDTH + B_WIDTH, D_MODEL))] + f_in,
        out_specs=pl.BlockSpec((tm, D_MODEL), row),
        out_shape=jax.ShapeDtypeStruct((N_ROWS, D_MODEL), F32),
        scratch_shapes=f_scratch + [pltpu.VMEM((tm + POOL_HALO, B_WIDTH), F32), pltpu.VMEM((POOL_HALO, B_WIDTH), F32)],
        compiler_params=_params(("arbitrary",)),
        name="ab_ffn",
    )(a, u, x, wg, ps, wo, *ffn_args)


def _c_ffn(att, x, wo, ffn_args):
    tm = ROW_TILE
    row = lambda i: (i, 0)
    f_in, f_scratch = _ffn_specs()
    return pl.pallas_call(
        _c_ffn_kernel,
        grid=(N_ROWS // tm,),
        in_specs=[pl.BlockSpec((tm, C_QW), row), pl.BlockSpec((tm, D_MODEL), row),
                  _const_spec((C_QW, D_MODEL))] + f_in,
        out_specs=pl.BlockSpec((tm, D_MODEL), row),
        out_shape=jax.ShapeDtypeStruct((N_ROWS, D_MODEL), F32),
        scratch_shapes=f_scratch,
        compiler_params=_params(("arbitrary",)),
        name="c_ffn",
    )(att, x, wo, *ffn_args)


def _alibi_slopes(n):
    return 2.0 ** (-8.0 * jnp.arange(1, n + 1, dtype=F32) / n)


def _block_diag_ones():
    seg = jnp.arange(NORM_SEG) // 64
    return (seg[:, None] == seg[None, :]).astype(BF16)


def _row(v):
    return v.reshape(1, -1).astype(F32)


def kernel(x, ab_norm, ab_w_in, a_q_norm, a_k_norm, a_lambda, a_sub_norm, b_w_group, b_scale, ab_w_out,
           c_norm, c_w_in, c_q_norm, c_k_norm, c_sinks, c_w_out, f_norm, f_w_up, f_conv, f_conv_b, f_w_down):
    bd = _block_diag_ones()
    h = x.reshape(N_ROWS, D_MODEL)

    a_slopes = _alibi_slopes(A_HEADS)
    ti = jnp.arange(ATT_TILE, dtype=F32)
    a_dtab = -a_slopes[:, None, None] * (ti[:, None] - ti[None, :])[None]
    c_slopes = _alibi_slopes(C_Q_HEADS)
    rel = C_QBLK + jnp.arange(C_QBLK)[:, None] - jnp.arange(2 * C_QBLK)[None, :]
    c_valid = (rel >= 0) & (rel < C_WINDOW)
    c_bias = jnp.where(c_valid[None], -c_slopes[:, None, None] * rel.astype(F32)[None], MASK_VALUE)

    for layer in range(DEPTH):
        ffn_args = (_row(f_norm[layer]), f_w_up[layer].astype(BF16), f_conv[layer].astype(F32),
                    _row(f_conv_b[layer]), f_w_down[layer].astype(BF16))
        if layer % 2 == 0:
            e = layer // 2
            lam_init = 0.8 - 0.6 * math.exp(-0.3 * layer)
            lp = a_lambda[e].astype(F32)
            lam = jnp.exp(jnp.sum(lp[0] * lp[1])) - jnp.exp(jnp.sum(lp[2] * lp[3])) + lam_init
            qg = _row(jnp.tile(a_q_norm[e], 2 * A_HEADS)) * (A_HEAD_DIM ** -0.5)
            kg = _row(jnp.tile(a_k_norm[e], 2 * A_HEADS))
            q, kaug, vaug, u = _ab_in(h, _row(ab_norm[e]), ab_w_in[e].astype(BF16), bd, qg, kg)
            subg = _row(a_sub_norm[e]) * (1.0 - lam_init)
            lam1 = lam.reshape(1)
            bound = (1.02 * A_HEAD_DIM ** 0.5) * jnp.max(jnp.abs(a_q_norm[e])) * jnp.max(jnp.abs(a_k_norm[e]))
            bound = bound.astype(F32).reshape(1)
            a_out = lax.cond(
                bound[0] <= SCORE_BOUND_LIMIT,
                lambda: _diff_attn_bounded(a_slopes, lam1, bound, q, kaug, vaug, subg),
                lambda: _diff_attn_online(a_slopes, lam1, q, kaug, vaug, a_dtab, subg))
            h = _ab_ffn(a_out, u, h, b_w_group[e].astype(BF16), _row(b_scale[e]), ab_w_out[e].astype(BF16), ffn_args)
        else:
            o = layer // 2
            w = c_w_in[o]
            hd = C_HEAD_DIM
            wk = w[:, C_QW:C_QW + C_KV_HEADS * hd]
            wv = w[:, C_QW + C_KV_HEADS * hd:]
            dup = lambda m: jnp.concatenate([m[:, :hd], m[:, :hd], m[:, hd:], m[:, hd:]], axis=1)
            w_cat = jnp.concatenate([w[:, :C_QW], dup(wk), dup(wv)], axis=1).astype(BF16)
            qg = _row(jnp.tile(c_q_norm[o], C_Q_HEADS)) * (C_HEAD_DIM ** -0.5)
            kg = _row(jnp.tile(c_k_norm[o], 2 * C_KV_HEADS))
            q, kdup, vaug = _c_in(h, _row(c_norm[o]), w_cat, bd, qg, kg)
            sinks = c_sinks[o].astype(F32)
            bound = (1.02 * C_HEAD_DIM ** 0.5) * jnp.max(jnp.abs(c_q_norm[o])) * jnp.max(jnp.abs(c_k_norm[o]))
            bound = bound.astype(F32)
            shift = jnp.maximum(bound, sinks)
            tab = c_bias - shift[:, None, None]
            tab = jnp.stack([tab, jnp.where(jnp.arange(2 * C_QBLK) < C_QBLK, MASK_VALUE, tab)])
            sterm = jnp.exp(sinks - shift)
            sink_b = jnp.broadcast_to(sinks[:, None, None], (C_Q_HEADS, 1, LANES))
            att = lax.cond(
                bound <= SCORE_BOUND_LIMIT,
                lambda: _swa_bounded(sterm, q, kdup, vaug, tab),
                lambda: _swa_online(q, kdup, vaug, c_bias, sink_b))
            h = _c_ffn(att, h, c_w_out[o].astype(BF16), ffn_args)
    return h.reshape(BATCH, SEQ, D_MODEL)
```

```python
import functools
import math

import jax
import jax.numpy as jnp
from jax import lax
from jax.experimental import pallas as pl
from jax.experimental.pallas import tpu as pltpu

D_MODEL = 1024
BATCH = 8
SEQ = 4096
DEPTH = 4
EPS = 1e-6
MASK_VALUE = -1e30

A_WIDTH = 512
A_HEAD_DIM = 64
A_HEADS = 4
A_VDIM = 2 * A_HEAD_DIM
B_WIDTH = 512
POOL_WINDOWS = (2, 4, 8, 16)
B_GROUP_DIM = 128
POOL_HALO = 16

C_HEAD_DIM = 64
C_Q_HEADS = 16
C_KV_HEADS = 2
C_GROUP = 8
C_WINDOW = 128
C_QBLK = 128
C_QW = C_Q_HEADS * C_HEAD_DIM
C_PAIRS = C_GROUP // 2

D_FF = 2816
CONV_W = 3
FF_CHUNK = 256
FF_NCHUNK = D_FF // FF_CHUNK
CONV_HALO = 8

LANES = 128
NORM_SEG = 256

ROW_TILE = 512
PROJ_ROW_TILE = 1024
ATT_TILE = 256
ATT_QTILE = 2 * ATT_TILE
ATT_KCHUNK = 512
SCORE_BOUND_LIMIT = 30.0
VMEM_LIMIT = 56 * 1024 * 1024

N_ROWS = BATCH * SEQ
F32 = jnp.float32
BF16 = jnp.bfloat16


def _params(sem, vmem=VMEM_LIMIT):
    return pltpu.CompilerParams(dimension_semantics=sem, vmem_limit_bytes=vmem)


def _const_spec(shape, layer=None):
    nd = len(shape)
    if layer is None:
        return pl.BlockSpec(shape, lambda *_: (0,) * nd, pipeline_mode=pl.Buffered(1))
    return pl.BlockSpec((None,) + tuple(shape), lambda *_: (layer,) + (0,) * nd, pipeline_mode=pl.Buffered(1))


def _rms_rows(x, g):
    ms = jnp.mean(x * x, axis=-1, keepdims=True)
    return x * lax.rsqrt(ms + EPS) * g


def _seg_norm(seg, bd, gain):
    ss = jnp.dot((seg * seg).astype(BF16), bd, preferred_element_type=F32)
    return seg * lax.rsqrt(ss * (1.0 / 64.0) + EPS) * gain


def _ab_in_kernel(x_ref, g_ref, w_ref, bd_ref, qg_ref, kg_ref, q_ref, k_ref, v_ref, u_ref):
    tm = PROJ_ROW_TILE
    xn = _rms_rows(x_ref[...], g_ref[...]).astype(BF16)
    bd = bd_ref[...]
    lane = lax.broadcasted_iota(jnp.int32, (tm, LANES), 1)
    pos = (pl.program_id(0) % (SEQ // tm)) * tm + lax.broadcasted_iota(jnp.int32, (tm, LANES), 0)
    kaug = jnp.where(lane == 0, pos >> 6, jnp.where(lane == 1, pos & 63, jnp.where(lane == 2, 1, 0)))
    kaug = kaug.astype(F32).astype(BF16)
    def finish(seg, is_q, cb):
        lo, hi = cb * NORM_SEG, (cb + 1) * NORM_SEG
        if is_q:
            q_ref[:, lo:hi] = _seg_norm(seg, bd, qg_ref[:, lo:hi]).astype(BF16)
            return
        kn = _seg_norm(seg, bd, kg_ref[:, lo:hi]).astype(BF16)
        for hh in range(NORM_SEG // A_VDIM):
            h = cb * (NORM_SEG // A_VDIM) + hh
            k_ref[:, h * 2 * A_VDIM:h * 2 * A_VDIM + A_VDIM] = kn[:, hh * A_VDIM:(hh + 1) * A_VDIM]
            k_ref[:, h * 2 * A_VDIM + A_VDIM:(h + 1) * 2 * A_VDIM] = kaug

    pending = None
    for cb in range(A_WIDTH // NORM_SEG):
        for is_q in (True, False):
            c0 = cb * NORM_SEG + (0 if is_q else A_WIDTH)
            seg = jnp.dot(xn, w_ref[:, c0:c0 + NORM_SEG], preferred_element_type=F32)
            if pending is not None:
                finish(*pending)
            pending = (seg, is_q, cb)
    v = jnp.dot(xn, w_ref[:, 2 * A_WIDTH:3 * A_WIDTH], preferred_element_type=F32).astype(BF16)
    finish(*pending)
    ones = jnp.ones((v.shape[0], A_VDIM), BF16)
    for h in range(A_HEADS):
        v_ref[:, h * 2 * A_VDIM:h * 2 * A_VDIM + A_VDIM] = v[:, h * A_VDIM:(h + 1) * A_VDIM]
        v_ref[:, h * 2 * A_VDIM + A_VDIM:(h + 1) * 2 * A_VDIM] = ones
    u_ref[...] = jnp.dot(xn, w_ref[:, 3 * A_WIDTH:], preferred_element_type=F32)


def _ab_in(x, g, w, e, bd, qg, kg):
    tm = PROJ_ROW_TILE
    row = lambda i: (i, 0)
    return pl.pallas_call(
        _ab_in_kernel,
        grid=(N_ROWS // tm,),
        in_specs=[pl.BlockSpec((tm, D_MODEL), row), _const_spec((1, D_MODEL)),
                  _const_spec((D_MODEL, 3 * A_WIDTH + B_WIDTH), e), _const_spec((NORM_SEG, NORM_SEG)),
                  _const_spec((1, A_WIDTH)), _const_spec((1, A_WIDTH))],
        out_specs=[pl.BlockSpec((tm, A_WIDTH), row), pl.BlockSpec((tm, 2 * A_WIDTH), row),
                   pl.BlockSpec((tm, 2 * A_WIDTH), row), pl.BlockSpec((tm, B_WIDTH), row)],
        out_shape=[jax.ShapeDtypeStruct((N_ROWS, A_WIDTH), BF16), jax.ShapeDtypeStruct((N_ROWS, 2 * A_WIDTH), BF16),
                   jax.ShapeDtypeStruct((N_ROWS, 2 * A_WIDTH), BF16), jax.ShapeDtypeStruct((N_ROWS, B_WIDTH), F32)],
        compiler_params=_params(("arbitrary",)),
        name="ab_in",
    )(x, g, w, bd, qg, kg)


def _diff_attn_kernel(slope_ref, lam_ref, q_ref, k_ref, v_ref, d_ref, sg_ref, o_ref, qs_ref, m_ref, acc_ref):
    T = ATT_TILE
    h = pl.program_id(1)
    i = pl.program_id(2)
    q = q_ref[...]
    lane = lax.broadcasted_iota(jnp.int32, q.shape, 1)
    zero = jnp.zeros_like(q)
    qs_ref[0:T, :] = jnp.where(lane < A_HEAD_DIM, q, zero)
    qs_ref[T:2 * T, :] = jnp.where(lane >= A_HEAD_DIM, q, zero)
    m_ref[...] = jnp.full(m_ref.shape, MASK_VALUE, F32)
    acc_ref[...] = jnp.zeros(acc_ref.shape, F32)
    slope = slope_ref[h]
    dtab = d_ref[...]

    def step(j, masked):
        koff = pl.multiple_of(j * T, T)
        kt = k_ref[pl.ds(koff, T), :]
        vt = v_ref[pl.ds(koff, T), :]
        s = lax.dot_general(qs_ref[...], kt, (((1,), (1,)), ((), ())), preferred_element_type=F32)
        bias = dtab
        if masked:
            r = lax.broadcasted_iota(jnp.int32, (T, T), 0)
            c = lax.broadcasted_iota(jnp.int32, (T, T), 1)
            bias = jnp.where(c > r, MASK_VALUE, dtab)
        s = s + jnp.concatenate([bias, bias], axis=0)
        off = slope * ((j - i) * T).astype(F32)
        m_prev = m_ref[...]
        m_new = jnp.maximum(m_prev, jnp.max(s, axis=1, keepdims=True) + off)
        alpha = jnp.exp(m_prev - m_new)
        shift = m_new - off
        p = jnp.exp(s - jnp.concatenate([shift] * (T // LANES), axis=1))
        pv = jnp.dot(p.astype(BF16), vt, preferred_element_type=F32)
        acc_ref[...] = acc_ref[...] * jnp.concatenate([alpha, alpha], axis=1) + pv
        m_ref[...] = m_new

    def body(j, carry):
        step(j, False)
        return carry

    lax.fori_loop(0, i, body, 0)
    step(i, True)

    acc = acc_ref[...]
    o0 = acc[0:T, 0:A_VDIM] / acc[0:T, A_VDIM:]
    o1 = acc[T:, 0:A_VDIM] / acc[T:, A_VDIM:]
    o = o0 - lam_ref[0] * o1
    o_ref[...] = _rms_rows(o, sg_ref[...]).astype(BF16)


def _diff_attn_online(slopes, lam, q, kaug, vaug, dtab, subg):
    T = ATT_TILE
    nq = SEQ // T
    smem = pl.BlockSpec(memory_space=pltpu.SMEM)
    return pl.pallas_call(
        _diff_attn_kernel,
        grid=(BATCH, A_HEADS, nq),
        in_specs=[smem, smem,
                  pl.BlockSpec((T, A_VDIM), lambda b, h, i: (b * nq + i, h)),
                  pl.BlockSpec((SEQ, A_VDIM), lambda b, h, i: (b, 2 * h)),
                  pl.BlockSpec((SEQ, 2 * A_VDIM), lambda b, h, i: (b, h)),
                  pl.BlockSpec((None, T, T), lambda b, h, i: (h, 0, 0)),
                  pl.BlockSpec((1, A_VDIM), lambda b, h, i: (0, 0))],
        out_specs=pl.BlockSpec((T, A_VDIM), lambda b, h, i: (b * nq + i, h)),
        out_shape=jax.ShapeDtypeStruct((N_ROWS, A_WIDTH), BF16),
        scratch_shapes=[pltpu.VMEM((2 * T, A_VDIM), BF16), pltpu.VMEM((2 * T, LANES), F32),
                        pltpu.VMEM((2 * T, 2 * A_VDIM), F32)],
        compiler_params=_params(("arbitrary", "arbitrary", "arbitrary")),
        name="diff_attn_online",
    )(slopes, lam, q, kaug, vaug, dtab, subg)


def _diff_attn_bounded_kernel(slope_ref, lam_ref, bound_ref, q_ref, k_ref, v_ref, sg_ref, o_ref):
    TQ, HB, C = ATT_QTILE, ATT_TILE, ATT_KCHUNK
    slope = slope_ref[pl.program_id(1)]
    lam = lam_ref[0]
    lane = lax.broadcasted_iota(jnp.int32, (TQ, A_VDIM), 1)
    row = lax.broadcasted_iota(jnp.int32, (TQ, A_VDIM), 0)
    zero = jnp.zeros((TQ, A_VDIM), BF16)
    r = lax.broadcasted_iota(jnp.int32, (HB, HB), 0)
    c = lax.broadcasted_iota(jnp.int32, (HB, HB), 1)
    tri = jnp.where(c > r, MASK_VALUE, 0.0).astype(F32)
    tri2 = jnp.concatenate([tri, tri], axis=0)
    mask_a = jnp.concatenate([tri2, jnp.zeros((2 * HB, HB), F32)], axis=0)
    nt = (((1,), (1,)), ((), ()))

    def pv(s, lo, hi):
        return jnp.dot(jnp.exp(s).astype(BF16), v_ref[lo:hi, :], preferred_element_type=F32)

    for i in range(SEQ // TQ):
        q = q_ref[i * TQ:(i + 1) * TQ, :]
        t = (i * TQ + row).astype(F32)
        row_shift = -(slope * t + bound_ref[0])
        aug = jnp.where(lane == 0, 64.0 * slope, jnp.where(lane == 1, slope, jnp.where(lane == 2, row_shift, 0.0)))
        aug = aug.astype(BF16)
        maps = (jnp.where(lane < A_HEAD_DIM, q, zero), jnp.where(lane >= A_HEAD_DIM, q, zero))
        qs = jnp.concatenate(
            [jnp.concatenate([maps[mp][blk * HB:(blk + 1) * HB, :], aug[blk * HB:(blk + 1) * HB, :]], axis=1)
             for blk in range(TQ // HB) for mp in range(2)], axis=0)
        ka, kb = i * TQ, i * TQ + HB
        s_a = lax.dot_general(qs, k_ref[ka:kb, :], nt, preferred_element_type=F32) + mask_a
        s_b = lax.dot_general(qs[2 * HB:, :], k_ref[kb:kb + HB, :], nt, preferred_element_type=F32) + tri2
        acc = pv(s_a, ka, kb) + jnp.concatenate([jnp.zeros((2 * HB, 2 * A_VDIM), F32), pv(s_b, kb, kb + HB)], axis=0)
        for j in range(i):
            s = lax.dot_general(qs, k_ref[j * C:(j + 1) * C, :], nt, preferred_element_type=F32)
            acc = acc + pv(s, j * C, (j + 1) * C)
        for blk in range(TQ // HB):
            a0 = acc[(2 * blk) * HB:(2 * blk + 1) * HB, :]
            a1 = acc[(2 * blk + 1) * HB:(2 * blk + 2) * HB, :]
            o = a0[:, 0:A_VDIM] / a0[:, A_VDIM:] - lam * (a1[:, 0:A_VDIM] / a1[:, A_VDIM:])
            o_ref[i * TQ + blk * HB:i * TQ + (blk + 1) * HB, :] = _rms_rows(o, sg_ref[...]).astype(BF16)


def _diff_attn_bounded(slopes, lam, bound, q, kaug, vaug, subg):
    smem = pl.BlockSpec(memory_space=pltpu.SMEM)
    bh = lambda b, h: (b, h)
    return pl.pallas_call(
        _diff_attn_bounded_kernel,
        grid=(BATCH, A_HEADS),
        in_specs=[smem, smem, smem,
                  pl.BlockSpec((SEQ, A_VDIM), bh), pl.BlockSpec((SEQ, 2 * A_VDIM), bh),
                  pl.BlockSpec((SEQ, 2 * A_VDIM), bh), pl.BlockSpec((1, A_VDIM), lambda b, h: (0, 0))],
        out_specs=pl.BlockSpec((SEQ, A_VDIM), bh),
        out_shape=jax.ShapeDtypeStruct((N_ROWS, A_WIDTH), BF16),
        compiler_params=_params(("arbitrary", "arbitrary")),
        name="diff_attn_bounded",
    )(slopes, lam, bound, q, kaug, vaug, subg)


def _ab_mix(a_ref, u_ref, x_ref, wg_ref, ps_ref, wo_ref, uext_ref, carry_ref):
    tm = ROW_TILE
    tiles_per_seq = SEQ // tm
    it = pl.program_id(0) % tiles_per_seq

    @pl.when(it == 0)
    def _():
        carry_ref[...] = jnp.zeros(carry_ref.shape, F32)

    u = u_ref[...]
    uext_ref[0:POOL_HALO, :] = carry_ref[...]
    uext_ref[POOL_HALO:, :] = u
    carry_ref[...] = u[tm - POOL_HALO:, :]
    pos = it * tm + lax.broadcasted_iota(jnp.int32, (tm, 1), 0)
    mixed = []
    for g, w in enumerate(POOL_WINDOWS):
        lo, hi = g * B_GROUP_DIM, (g + 1) * B_GROUP_DIM
        ug = u[:, lo:hi]
        win = ug
        for d in range(1, w):
            win = win + uext_ref[pl.ds(POOL_HALO - d, tm), lo:hi]
        cnt = jnp.minimum(pos + 1, w).astype(F32)
        pooled = win / cnt - ug
        mg = jnp.dot(pooled.astype(BF16), wg_ref[g], preferred_element_type=F32)
        mixed.append((mg * ps_ref[:, lo:hi]).astype(BF16))
    mix = jnp.concatenate([a_ref[...]] + mixed, axis=1)
    return x_ref[...] + jnp.dot(mix, wo_ref[...], preferred_element_type=F32)


def _c_in_kernel(x_ref, g_ref, w_ref, bd_ref, qg_ref, kg_ref, q_ref, k_ref, v_ref):
    xn = _rms_rows(x_ref[...], g_ref[...]).astype(BF16)
    bd = bd_ref[...]
    def finish(seg, cb):
        lo, hi = cb * NORM_SEG, (cb + 1) * NORM_SEG
        if cb < C_QW // NORM_SEG:
            q_ref[:, lo:hi] = _seg_norm(seg, bd, qg_ref[:, lo:hi]).astype(BF16)
        else:
            k_ref[...] = _seg_norm(seg, bd, kg_ref[...]).astype(BF16)

    pending = None
    for cb in range(C_QW // NORM_SEG + 1):
        seg = jnp.dot(xn, w_ref[:, cb * NORM_SEG:(cb + 1) * NORM_SEG], preferred_element_type=F32)
        if pending is not None:
            finish(*pending)
        pending = (seg, cb)
    v = jnp.dot(xn, w_ref[:, C_QW + NORM_SEG:], preferred_element_type=F32).astype(BF16)
    finish(*pending)
    ones = jnp.ones((v.shape[0], LANES), BF16)
    for g in range(C_KV_HEADS):
        v_ref[:, g * 2 * LANES:g * 2 * LANES + LANES] = v[:, g * LANES:(g + 1) * LANES]
        v_ref[:, g * 2 * LANES + LANES:(g + 1) * 2 * LANES] = ones


def _c_in(x, g, w, o, bd, qg, kg):
    tm = PROJ_ROW_TILE
    row = lambda i: (i, 0)
    kvw = 2 * C_KV_HEADS * C_HEAD_DIM
    return pl.pallas_call(
        _c_in_kernel,
        grid=(N_ROWS // tm,),
        in_specs=[pl.BlockSpec((tm, D_MODEL), row), _const_spec((1, D_MODEL)),
                  _const_spec((D_MODEL, C_QW + 2 * kvw), o), _const_spec((NORM_SEG, NORM_SEG)),
                  _const_spec((1, C_QW)), _const_spec((1, kvw))],
        out_specs=[pl.BlockSpec((tm, C_QW), row), pl.BlockSpec((tm, kvw), row), pl.BlockSpec((tm, 2 * kvw), row)],
        out_shape=[jax.ShapeDtypeStruct((N_ROWS, C_QW), BF16), jax.ShapeDtypeStruct((N_ROWS, kvw), BF16),
                   jax.ShapeDtypeStruct((N_ROWS, 2 * kvw), BF16)],
        compiler_params=_params(("arbitrary",)),
        name="c_in",
    )(x, g, w, bd, qg, kg)


def _swa_kernel(q_ref, kp_ref, kc_ref, vp_ref, vc_ref, bias_ref, sink_ref, o_ref):
    Q = C_QBLK
    i = pl.program_id(1)
    q = q_ref[...]
    lane = lax.broadcasted_iota(jnp.int32, (Q, LANES), 1)
    first_half = lane < C_HEAD_DIM
    zero = jnp.zeros((Q, LANES), BF16)
    rows = []
    for p in range(C_PAIRS):
        qp = q[:, p * LANES:(p + 1) * LANES]
        rows.append(jnp.where(first_half, qp, zero))
        rows.append(jnp.where(first_half, zero, qp))
    qs = jnp.concatenate(rows, axis=0)
    k2 = jnp.concatenate([kp_ref[...], kc_ref[...]], axis=0)
    v2 = jnp.concatenate([vp_ref[...], vc_ref[...]], axis=0)
    s = lax.dot_general(qs, k2, (((1,), (1,)), ((), ())), preferred_element_type=F32)
    s = s.reshape(C_GROUP, Q, 2 * Q) + bias_ref[...]
    col = lax.broadcasted_iota(jnp.int32, (C_GROUP, Q, 2 * Q), 2)
    s = jnp.where(jnp.logical_and(i == 0, col < Q), MASK_VALUE, s)
    sink = sink_ref[...][:, :, 0:1]
    m = jnp.maximum(jnp.max(s, axis=-1, keepdims=True), sink)
    p = jnp.exp(s - m)
    denom = jnp.sum(p, axis=-1, keepdims=True) + jnp.exp(sink - m)
    pn = (p / denom).reshape(C_GROUP * Q, 2 * Q).astype(BF16)
    o = jnp.dot(pn, v2, preferred_element_type=F32)
    for p_ in range(C_PAIRS):
        oa = o[(2 * p_) * Q:(2 * p_ + 1) * Q, :]
        ob = o[(2 * p_ + 1) * Q:(2 * p_ + 2) * Q, :]
        o_ref[:, p_ * LANES:(p_ + 1) * LANES] = jnp.where(first_half, oa, ob).astype(BF16)


def _swa_online(q, kdup, vaug, bias, sink):
    Q = C_QBLK
    nb = SEQ // Q
    gw = C_GROUP * C_HEAD_DIM
    cur = lambda b, i, g: (b * nb + i, g)
    prev = lambda b, i, g: (b * nb + jnp.maximum(i - 1, 0), g)
    vcur = lambda b, i, g: (b * nb + i, 2 * g)
    vprev = lambda b, i, g: (b * nb + jnp.maximum(i - 1, 0), 2 * g)
    return pl.pallas_call(
        _swa_kernel,
        grid=(BATCH, nb, C_KV_HEADS),
        in_specs=[pl.BlockSpec((Q, gw), cur),
                  pl.BlockSpec((Q, LANES), prev), pl.BlockSpec((Q, LANES), cur),
                  pl.BlockSpec((Q, LANES), vprev), pl.BlockSpec((Q, LANES), vcur),
                  pl.BlockSpec((C_GROUP, Q, 2 * Q), lambda b, i, g: (g, 0, 0)),
                  pl.BlockSpec((C_GROUP, 1, LANES), lambda b, i, g: (g, 0, 0))],
        out_specs=pl.BlockSpec((Q, gw), cur),
        out_shape=jax.ShapeDtypeStruct((N_ROWS, C_QW), BF16),
        compiler_params=_params(("arbitrary", "arbitrary", "arbitrary")),
        name="swa_online",
    )(q, kdup, kdup, vaug, vaug, bias, sink)


def _swa_bounded_kernel(sterm_ref, q_ref, kp_ref, kc_ref, vp_ref, vc_ref, tab_ref, o_ref):
    Q = C_QBLK
    first = jnp.where(pl.program_id(1) == 0, 1, 0)
    lane = lax.broadcasted_iota(jnp.int32, (Q, LANES), 1)
    first_half = lane < C_HEAD_DIM
    zero = jnp.zeros((Q, LANES), BF16)
    nt = (((1,), (1,)), ((), ()))
    for g in range(C_KV_HEADS):
        kcol = slice(g * LANES, (g + 1) * LANES)
        vcol = slice(g * 2 * LANES, (g + 1) * 2 * LANES)
        for sub in range(2):
            rows = []
            for p in range(C_PAIRS):
                c0 = g * C_GROUP * C_HEAD_DIM + p * LANES
                qp = q_ref[sub * Q:(sub + 1) * Q, c0:c0 + LANES]
                rows.append(jnp.where(first_half, qp, zero))
                rows.append(jnp.where(first_half, zero, qp))
            qs = jnp.concatenate(rows, axis=0)
            if sub == 0:
                k2 = jnp.concatenate([kp_ref[:, kcol], kc_ref[0:Q, kcol]], axis=0)
                v3 = jnp.concatenate([vp_ref[:, vcol], vc_ref[0:Q, vcol]], axis=0)
                tab = tab_ref[first, g * C_GROUP:(g + 1) * C_GROUP]
            else:
                k2 = kc_ref[:, kcol]
                v3 = vc_ref[:, vcol]
                tab = tab_ref[0, g * C_GROUP:(g + 1) * C_GROUP]
            s = lax.dot_general(qs, k2, nt, preferred_element_type=F32)
            p_ = jnp.exp(s.reshape(C_GROUP, Q, 2 * Q) + tab).astype(BF16).reshape(C_GROUP * Q, 2 * Q)
            o = jnp.dot(p_, v3, preferred_element_type=F32)
            for pr in range(C_PAIRS):
                outs = []
                for hh in (2 * pr, 2 * pr + 1):
                    blk = o[hh * Q:(hh + 1) * Q, :]
                    outs.append(blk[:, 0:LANES] / (blk[:, LANES:] + sterm_ref[g * C_GROUP + hh]))
                c0 = g * C_GROUP * C_HEAD_DIM + pr * LANES
                o_ref[sub * Q:(sub + 1) * Q, c0:c0 + LANES] = jnp.where(first_half, outs[0], outs[1]).astype(BF16)


def _swa_bounded(sterm, q, kdup, vaug, tab):
    Q = C_QBLK
    nb2 = SEQ // (2 * Q)
    kvw = 2 * C_KV_HEADS * C_HEAD_DIM
    cur = lambda b, i: (b * nb2 + i, 0)
    prev = lambda b, i: (b * 2 * nb2 + jnp.maximum(2 * i - 1, 0), 0)
    return pl.pallas_call(
        _swa_bounded_kernel,
        grid=(BATCH, nb2),
        in_specs=[pl.BlockSpec(memory_space=pltpu.SMEM),
                  pl.BlockSpec((2 * Q, C_QW), cur),
                  pl.BlockSpec((Q, kvw), prev), pl.BlockSpec((2 * Q, kvw), cur),
                  pl.BlockSpec((Q, 2 * kvw), prev), pl.BlockSpec((2 * Q, 2 * kvw), cur),
                  _const_spec((2, C_Q_HEADS, Q, 2 * Q))],
        out_specs=pl.BlockSpec((2 * Q, C_QW), cur),
        out_shape=jax.ShapeDtypeStruct((N_ROWS, C_QW), BF16),
        compiler_params=_params(("arbitrary", "arbitrary")),
        name="swa_bounded",
    )(sterm, q, kdup, kdup, vaug, vaug, tab)


def _ffn_tail(x, g_ref, wup_ref, cw_ref, cb_ref, wd_ref, o_ref, hs_ref, carry_ref, act_ref):
    tm = ROW_TILE
    H = CONV_HALO

    @pl.when(pl.program_id(0) % (SEQ // tm) == 0)
    def _():
        carry_ref[...] = jnp.zeros(carry_ref.shape, F32)

    xn = _rms_rows(x, g_ref[...]).astype(BF16)

    def conv(c, part):
        cols = slice(part * D_FF + c * FF_CHUNK, part * D_FF + (c + 1) * FF_CHUNK)
        h = jnp.dot(xn, wup_ref[:, cols], preferred_element_type=F32)
        buf = hs_ref.at[c % 2, part]
        buf[0:H, :] = carry_ref[part, c]
        buf[H:, :] = h
        carry_ref[part, c] = h[tm - H:, :]
        return (cb_ref[:, cols] + cw_ref[2:3, cols] * h + cw_ref[1:2, cols] * buf[pl.ds(H - 1, tm), :]
                + cw_ref[0:1, cols] * buf[pl.ds(H - 2, tm), :])

    for c in range(FF_NCHUNK):
        gate = conv(c, 0)
        up = conv(c, 1)
        act = gate / (1.0 + jnp.exp(-gate)) * up
        act_ref[:, c * FF_CHUNK:(c + 1) * FF_CHUNK] = act.astype(BF16)
    o_ref[...] = x + jnp.dot(act_ref[...], wd_ref[...], preferred_element_type=F32)


def _ab_ffn_kernel(a_ref, u_ref, x_ref, wg_ref, ps_ref, wo_ref, g_ref, wup_ref, cw_ref, cb_ref, wd_ref, o_ref,
                   hs_ref, carry_ref, act_ref, uext_ref, ucarry_ref):
    x = _ab_mix(a_ref, u_ref, x_ref, wg_ref, ps_ref, wo_ref, uext_ref, ucarry_ref)
    _ffn_tail(x, g_ref, wup_ref, cw_ref, cb_ref, wd_ref, o_ref, hs_ref, carry_ref, act_ref)


def _c_ffn_kernel(att_ref, x_ref, wo_ref, g_ref, wup_ref, cw_ref, cb_ref, wd_ref, o_ref, hs_ref, carry_ref, act_ref):
    x = x_ref[...] + jnp.dot(att_ref[...], wo_ref[...], preferred_element_type=F32)
    _ffn_tail(x, g_ref, wup_ref, cw_ref, cb_ref, wd_ref, o_ref, hs_ref, carry_ref, act_ref)


def _ffn_specs(layer):
    tm = ROW_TILE
    in_specs = [_const_spec((1, D_MODEL), layer), _const_spec((D_MODEL, 2 * D_FF), layer),
                _const_spec((CONV_W, 2 * D_FF), layer), _const_spec((1, 2 * D_FF), layer),
                _const_spec((D_FF, D_MODEL), layer)]
    scratch = [pltpu.VMEM((2, 2, tm + CONV_HALO, FF_CHUNK), F32),
               pltpu.VMEM((2, FF_NCHUNK, CONV_HALO, FF_CHUNK), F32),
               pltpu.VMEM((tm, D_FF), BF16)]
    return in_specs, scratch


def _ab_ffn(a, u, x, wg, ps, wo, e, ffn_args, layer):
    tm = ROW_TILE
    row = lambda i: (i, 0)
    f_in, f_scratch = _ffn_specs(layer)
    return pl.pallas_call(
        _ab_ffn_kernel,
        grid=(N_ROWS // tm,),
        in_specs=[pl.BlockSpec((tm, A_WIDTH), row), pl.BlockSpec((tm, B_WIDTH), row),
                  pl.BlockSpec((tm, D_MODEL), row),
                  _const_spec((len(POOL_WINDOWS), B_GROUP_DIM, B_GROUP_DIM), e),
                  _const_spec((1, B_WIDTH)), _const_spec((A_WIDTH + B_WIDTH, D_MODEL), e)] + f_in,
        out_specs=pl.BlockSpec((tm, D_MODEL), row),
        out_shape=jax.ShapeDtypeStruct((N_ROWS, D_MODEL), F32),
        scratch_shapes=f_scratch + [pltpu.VMEM((tm + POOL_HALO, B_WIDTH), F32), pltpu.VMEM((POOL_HALO, B_WIDTH), F32)],
        compiler_params=_params(("arbitrary",)),
        name="ab_ffn",
    )(a, u, x, wg, ps, wo, *ffn_args)


def _c_ffn(att, x, wo, o, ffn_args, layer):
    tm = ROW_TILE
    row = lambda i: (i, 0)
    f_in, f_scratch = _ffn_specs(layer)
    return pl.pallas_call(
        _c_ffn_kernel,
        grid=(N_ROWS // tm,),
        in_specs=[pl.BlockSpec((tm, C_QW), row), pl.BlockSpec((tm, D_MODEL), row),
                  _const_spec((C_QW, D_MODEL), o)] + f_in,
        out_specs=pl.BlockSpec((tm, D_MODEL), row),
        out_shape=jax.ShapeDtypeStruct((N_ROWS, D_MODEL), F32),
        scratch_shapes=f_scratch,
        compiler_params=_params(("arbitrary",)),
        name="c_ffn",
    )(att, x, wo, *ffn_args)


def _alibi_slopes(n):
    return 2.0 ** (-8.0 * jnp.arange(1, n + 1, dtype=F32) / n)


def _block_diag_ones():
    seg = jnp.arange(NORM_SEG) // 64
    return (seg[:, None] == seg[None, :]).astype(BF16)


def _row(v):
    return v.reshape(1, -1).astype(F32)


def kernel(x, ab_norm, ab_w_in, a_q_norm, a_k_norm, a_lambda, a_sub_norm, b_w_group, b_scale, ab_w_out,
           c_norm, c_w_in, c_q_norm, c_k_norm, c_sinks, c_w_out, f_norm, f_w_up, f_conv, f_conv_b, f_w_down):
    bd = _block_diag_ones()
    h = x.reshape(N_ROWS, D_MODEL)

    a_slopes = _alibi_slopes(A_HEADS)
    ti = jnp.arange(ATT_TILE, dtype=F32)
    a_dtab = -a_slopes[:, None, None] * (ti[:, None] - ti[None, :])[None]
    c_slopes = _alibi_slopes(C_Q_HEADS)
    rel = C_QBLK + jnp.arange(C_QBLK)[:, None] - jnp.arange(2 * C_QBLK)[None, :]
    c_valid = (rel >= 0) & (rel < C_WINDOW)
    c_bias = jnp.where(c_valid[None], -c_slopes[:, None, None] * rel.astype(F32)[None], MASK_VALUE)

    ffn_args = (f_norm.astype(F32)[:, None, :], f_w_up.astype(BF16), f_conv.astype(F32),
                f_conv_b.astype(F32)[:, None, :], f_w_down.astype(BF16))
    ab_w_in_b, ab_w_out_b, b_w_group_b = ab_w_in.astype(BF16), ab_w_out.astype(BF16), b_w_group.astype(BF16)
    c_w_out_b = c_w_out.astype(BF16)
    hd = C_HEAD_DIM
    kcols = [C_QW + g * hd for g in range(C_KV_HEADS) for _ in range(2)]
    vcols = [c0 + C_KV_HEADS * hd for c0 in kcols]
    c_w_cat = jnp.concatenate([c_w_in[:, :, :C_QW]] + [c_w_in[:, :, c0:c0 + hd] for c0 in kcols + vcols],
                              axis=2).astype(BF16)

    for layer in range(DEPTH):
        if layer % 2 == 0:
            e = layer // 2
            lam_init = 0.8 - 0.6 * math.exp(-0.3 * layer)
            lp = a_lambda[e].astype(F32)
            lam = jnp.exp(jnp.sum(lp[0] * lp[1])) - jnp.exp(jnp.sum(lp[2] * lp[3])) + lam_init
            qg = _row(jnp.tile(a_q_norm[e], 2 * A_HEADS)) * (A_HEAD_DIM ** -0.5)
            kg = _row(jnp.tile(a_k_norm[e], 2 * A_HEADS))
            q, kaug, vaug, u = _ab_in(h, _row(ab_norm[e]), ab_w_in_b, e, bd, qg, kg)
            subg = _row(a_sub_norm[e]) * (1.0 - lam_init)
            lam1 = lam.reshape(1)
            bound = (1.02 * A_HEAD_DIM ** 0.5) * jnp.max(jnp.abs(a_q_norm[e])) * jnp.max(jnp.abs(a_k_norm[e]))
            bound = bound.astype(F32).reshape(1)
            a_out = lax.cond(
                bound[0] <= SCORE_BOUND_LIMIT,
                lambda: _diff_attn_bounded(a_slopes, lam1, bound, q, kaug, vaug, subg),
                lambda: _diff_attn_online(a_slopes, lam1, q, kaug, vaug, a_dtab, subg))
            h = _ab_ffn(a_out, u, h, b_w_group_b, _row(b_scale[e]), ab_w_out_b, e, ffn_args, layer)
        else:
            o = layer // 2
            qg = _row(jnp.tile(c_q_norm[o], C_Q_HEADS)) * (C_HEAD_DIM ** -0.5)
            kg = _row(jnp.tile(c_k_norm[o], 2 * C_KV_HEADS))
            q, kdup, vaug = _c_in(h, _row(c_norm[o]), c_w_cat, o, bd, qg, kg)
            sinks = c_sinks[o].astype(F32)
            bound = (1.02 * C_HEAD_DIM ** 0.5) * jnp.max(jnp.abs(c_q_norm[o])) * jnp.max(jnp.abs(c_k_norm[o]))
            bound = bound.astype(F32)
            shift = jnp.maximum(bound, sinks)
            tab = c_bias - shift[:, None, None]
            tab = jnp.stack([tab, jnp.where(jnp.arange(2 * C_QBLK) < C_QBLK, MASK_VALUE, tab)])
            sterm = jnp.exp(sinks - shift)
            sink_b = jnp.broadcast_to(sinks[:, None, None], (C_Q_HEADS, 1, LANES))
            att = lax.cond(
                bound <= SCORE_BOUND_LIMIT,
                lambda: _swa_bounded(sterm, q, kdup, vaug, tab),
                lambda: _swa_online(q, kdup, vaug, c_bias, sink_b))
            h = _c_ffn(att, h, c_w_out_b, o, ffn_args, layer)
    return h.reshape(BATCH, SEQ, D_MODEL)
```

```python
import functools
import math

import jax
import jax.numpy as jnp
from jax import lax
from jax.experimental import pallas as pl
from jax.experimental.pallas import tpu as pltpu

D_MODEL = 1024
BATCH = 8
SEQ = 4096
DEPTH = 4
EPS = 1e-6
MASK_VALUE = -1e30

A_WIDTH = 512
A_HEAD_DIM = 64
A_HEADS = 4
A_VDIM = 2 * A_HEAD_DIM
B_WIDTH = 512
POOL_WINDOWS = (2, 4, 8, 16)
B_GROUP_DIM = 128
POOL_HALO = 16

C_HEAD_DIM = 64
C_Q_HEADS = 16
C_KV_HEADS = 2
C_GROUP = 8
C_WINDOW = 128
C_QBLK = 128
SWA_BLOCKS = 4
C_QW = C_Q_HEADS * C_HEAD_DIM
C_PAIRS = C_GROUP // 2

D_FF = 2816
CONV_W = 3
FF_CHUNK = 256
FF_NCHUNK = D_FF // FF_CHUNK
CONV_HALO = 8

LANES = 128
NORM_SEG = 256

ROW_TILE = 512
PROJ_ROW_TILE = 1024
ATT_TILE = 256
ATT_QTILE = 2 * ATT_TILE
ATT_KCHUNK = 512
SCORE_BOUND_LIMIT = 30.0
VMEM_LIMIT = 56 * 1024 * 1024

N_ROWS = BATCH * SEQ
F32 = jnp.float32
BF16 = jnp.bfloat16


def _params(sem, vmem=VMEM_LIMIT):
    return pltpu.CompilerParams(dimension_semantics=sem, vmem_limit_bytes=vmem)


def _const_spec(shape, layer=None):
    nd = len(shape)
    if layer is None:
        return pl.BlockSpec(shape, lambda *_: (0,) * nd, pipeline_mode=pl.Buffered(1))
    return pl.BlockSpec((None,) + tuple(shape), lambda *_: (layer,) + (0,) * nd, pipeline_mode=pl.Buffered(1))


def _rms_rows(x, g):
    ms = jnp.mean(x * x, axis=-1, keepdims=True)
    return x * lax.rsqrt(ms + EPS) * g


def _seg_norm(seg, bd, gain):
    ss = jnp.dot((seg * seg).astype(BF16), bd, preferred_element_type=F32)
    return seg * lax.rsqrt(ss * (1.0 / 64.0) + EPS) * gain


def _ab_in_kernel(x_ref, g_ref, w_ref, bd_ref, qg_ref, kg_ref, q_ref, k_ref, v_ref, u_ref):
    tm = PROJ_ROW_TILE
    xn = _rms_rows(x_ref[...], g_ref[...]).astype(BF16)
    bd = bd_ref[...]
    lane = lax.broadcasted_iota(jnp.int32, (tm, LANES), 1)
    pos = (pl.program_id(0) % (SEQ // tm)) * tm + lax.broadcasted_iota(jnp.int32, (tm, LANES), 0)
    kaug = jnp.where(lane == 0, pos >> 6, jnp.where(lane == 1, pos & 63, jnp.where(lane == 2, 1, 0)))
    kaug = kaug.astype(F32).astype(BF16)
    def finish(seg, is_q, cb):
        lo, hi = cb * NORM_SEG, (cb + 1) * NORM_SEG
        if is_q:
            q_ref[:, lo:hi] = _seg_norm(seg, bd, qg_ref[:, lo:hi]).astype(BF16)
            return
        kn = _seg_norm(seg, bd, kg_ref[:, lo:hi]).astype(BF16)
        for hh in range(NORM_SEG // A_VDIM):
            h = cb * (NORM_SEG // A_VDIM) + hh
            k_ref[:, h * 2 * A_VDIM:h * 2 * A_VDIM + A_VDIM] = kn[:, hh * A_VDIM:(hh + 1) * A_VDIM]
            k_ref[:, h * 2 * A_VDIM + A_VDIM:(h + 1) * 2 * A_VDIM] = kaug

    pending = None
    for cb in range(A_WIDTH // NORM_SEG):
        for is_q in (True, False):
            c0 = cb * NORM_SEG + (0 if is_q else A_WIDTH)
            seg = jnp.dot(xn, w_ref[:, c0:c0 + NORM_SEG], preferred_element_type=F32)
            if pending is not None:
                finish(*pending)
            pending = (seg, is_q, cb)
    v = jnp.dot(xn, w_ref[:, 2 * A_WIDTH:3 * A_WIDTH], preferred_element_type=F32).astype(BF16)
    finish(*pending)
    ones = jnp.ones((v.shape[0], A_VDIM), BF16)
    for h in range(A_HEADS):
        v_ref[:, h * 2 * A_VDIM:h * 2 * A_VDIM + A_VDIM] = v[:, h * A_VDIM:(h + 1) * A_VDIM]
        v_ref[:, h * 2 * A_VDIM + A_VDIM:(h + 1) * 2 * A_VDIM] = ones
    u_ref[...] = jnp.dot(xn, w_ref[:, 3 * A_WIDTH:], preferred_element_type=F32)


def _ab_in(x, g, w, e, bd, qg, kg):
    tm = PROJ_ROW_TILE
    row = lambda i: (i, 0)
    return pl.pallas_call(
        _ab_in_kernel,
        grid=(N_ROWS // tm,),
        in_specs=[pl.BlockSpec((tm, D_MODEL), row), _const_spec((1, D_MODEL)),
                  _const_spec((D_MODEL, 3 * A_WIDTH + B_WIDTH), e), _const_spec((NORM_SEG, NORM_SEG)),
                  _const_spec((1, A_WIDTH)), _const_spec((1, A_WIDTH))],
        out_specs=[pl.BlockSpec((tm, A_WIDTH), row), pl.BlockSpec((tm, 2 * A_WIDTH), row),
                   pl.BlockSpec((tm, 2 * A_WIDTH), row), pl.BlockSpec((tm, B_WIDTH), row)],
        out_shape=[jax.ShapeDtypeStruct((N_ROWS, A_WIDTH), BF16), jax.ShapeDtypeStruct((N_ROWS, 2 * A_WIDTH), BF16),
                   jax.ShapeDtypeStruct((N_ROWS, 2 * A_WIDTH), BF16), jax.ShapeDtypeStruct((N_ROWS, B_WIDTH), F32)],
        compiler_params=_params(("arbitrary",)),
        name="ab_in",
    )(x, g, w, bd, qg, kg)


def _diff_attn_kernel(slope_ref, lam_ref, q_ref, k_ref, v_ref, d_ref, sg_ref, o_ref, qs_ref, m_ref, acc_ref):
    T = ATT_TILE
    h = pl.program_id(1)
    i = pl.program_id(2)
    q = q_ref[...]
    lane = lax.broadcasted_iota(jnp.int32, q.shape, 1)
    zero = jnp.zeros_like(q)
    qs_ref[0:T, :] = jnp.where(lane < A_HEAD_DIM, q, zero)
    qs_ref[T:2 * T, :] = jnp.where(lane >= A_HEAD_DIM, q, zero)
    m_ref[...] = jnp.full(m_ref.shape, MASK_VALUE, F32)
    acc_ref[...] = jnp.zeros(acc_ref.shape, F32)
    slope = slope_ref[h]
    dtab = d_ref[...]

    def step(j, masked):
        koff = pl.multiple_of(j * T, T)
        kt = k_ref[pl.ds(koff, T), :]
        vt = v_ref[pl.ds(koff, T), :]
        s = lax.dot_general(qs_ref[...], kt, (((1,), (1,)), ((), ())), preferred_element_type=F32)
        bias = dtab
        if masked:
            r = lax.broadcasted_iota(jnp.int32, (T, T), 0)
            c = lax.broadcasted_iota(jnp.int32, (T, T), 1)
            bias = jnp.where(c > r, MASK_VALUE, dtab)
        s = s + jnp.concatenate([bias, bias], axis=0)
        off = slope * ((j - i) * T).astype(F32)
        m_prev = m_ref[...]
        m_new = jnp.maximum(m_prev, jnp.max(s, axis=1, keepdims=True) + off)
        alpha = jnp.exp(m_prev - m_new)
        shift = m_new - off
        p = jnp.exp(s - jnp.concatenate([shift] * (T // LANES), axis=1))
        pv = jnp.dot(p.astype(BF16), vt, preferred_element_type=F32)
        acc_ref[...] = acc_ref[...] * jnp.concatenate([alpha, alpha], axis=1) + pv
        m_ref[...] = m_new

    def body(j, carry):
        step(j, False)
        return carry

    lax.fori_loop(0, i, body, 0)
    step(i, True)

    acc = acc_ref[...]
    o0 = acc[0:T, 0:A_VDIM] / acc[0:T, A_VDIM:]
    o1 = acc[T:, 0:A_VDIM] / acc[T:, A_VDIM:]
    o = o0 - lam_ref[0] * o1
    o_ref[...] = _rms_rows(o, sg_ref[...]).astype(BF16)


def _diff_attn_online(slopes, lam, q, kaug, vaug, dtab, subg):
    T = ATT_TILE
    nq = SEQ // T
    smem = pl.BlockSpec(memory_space=pltpu.SMEM)
    return pl.pallas_call(
        _diff_attn_kernel,
        grid=(BATCH, A_HEADS, nq),
        in_specs=[smem, smem,
                  pl.BlockSpec((T, A_VDIM), lambda b, h, i: (b * nq + i, h)),
                  pl.BlockSpec((SEQ, A_VDIM), lambda b, h, i: (b, 2 * h)),
                  pl.BlockSpec((SEQ, 2 * A_VDIM), lambda b, h, i: (b, h)),
                  pl.BlockSpec((None, T, T), lambda b, h, i: (h, 0, 0)),
                  pl.BlockSpec((1, A_VDIM), lambda b, h, i: (0, 0))],
        out_specs=pl.BlockSpec((T, A_VDIM), lambda b, h, i: (b * nq + i, h)),
        out_shape=jax.ShapeDtypeStruct((N_ROWS, A_WIDTH), BF16),
        scratch_shapes=[pltpu.VMEM((2 * T, A_VDIM), BF16), pltpu.VMEM((2 * T, LANES), F32),
                        pltpu.VMEM((2 * T, 2 * A_VDIM), F32)],
        compiler_params=_params(("arbitrary", "arbitrary", "arbitrary")),
        name="diff_attn_online",
    )(slopes, lam, q, kaug, vaug, dtab, subg)


def _diff_attn_bounded_kernel(slope_ref, lam_ref, bound_ref, q_ref, k_ref, v_ref, sg_ref, o_ref):
    TQ, HB, C = ATT_QTILE, ATT_TILE, ATT_KCHUNK
    slope = slope_ref[pl.program_id(1)]
    lam = lam_ref[0]
    lane = lax.broadcasted_iota(jnp.int32, (TQ, A_VDIM), 1)
    row = lax.broadcasted_iota(jnp.int32, (TQ, A_VDIM), 0)
    zero = jnp.zeros((TQ, A_VDIM), BF16)
    r = lax.broadcasted_iota(jnp.int32, (HB, HB), 0)
    c = lax.broadcasted_iota(jnp.int32, (HB, HB), 1)
    tri = jnp.where(c > r, MASK_VALUE, 0.0).astype(F32)
    tri2 = jnp.concatenate([tri, tri], axis=0)
    mask_a = jnp.concatenate([tri2, jnp.zeros((2 * HB, HB), F32)], axis=0)
    nt = (((1,), (1,)), ((), ()))

    def pv(s, lo, hi):
        return jnp.dot(jnp.exp(s).astype(BF16), v_ref[lo:hi, :], preferred_element_type=F32)

    for i in range(SEQ // TQ):
        q = q_ref[i * TQ:(i + 1) * TQ, :]
        t = (i * TQ + row).astype(F32)
        row_shift = -(slope * t + bound_ref[0])
        aug = jnp.where(lane == 0, 64.0 * slope, jnp.where(lane == 1, slope, jnp.where(lane == 2, row_shift, 0.0)))
        aug = aug.astype(BF16)
        maps = (jnp.where(lane < A_HEAD_DIM, q, zero), jnp.where(lane >= A_HEAD_DIM, q, zero))
        qs = jnp.concatenate(
            [jnp.concatenate([maps[mp][blk * HB:(blk + 1) * HB, :], aug[blk * HB:(blk + 1) * HB, :]], axis=1)
             for blk in range(TQ // HB) for mp in range(2)], axis=0)
        ka, kb = i * TQ, i * TQ + HB
        s_a = lax.dot_general(qs, k_ref[ka:kb, :], nt, preferred_element_type=F32) + mask_a
        s_b = lax.dot_general(qs[2 * HB:, :], k_ref[kb:kb + HB, :], nt, preferred_element_type=F32) + tri2
        acc = pv(s_a, ka, kb) + jnp.concatenate([jnp.zeros((2 * HB, 2 * A_VDIM), F32), pv(s_b, kb, kb + HB)], axis=0)
        for j in range(i):
            s = lax.dot_general(qs, k_ref[j * C:(j + 1) * C, :], nt, preferred_element_type=F32)
            acc = acc + pv(s, j * C, (j + 1) * C)
        for blk in range(TQ // HB):
            a0 = acc[(2 * blk) * HB:(2 * blk + 1) * HB, :]
            a1 = acc[(2 * blk + 1) * HB:(2 * blk + 2) * HB, :]
            o = a0[:, 0:A_VDIM] / a0[:, A_VDIM:] - lam * (a1[:, 0:A_VDIM] / a1[:, A_VDIM:])
            o_ref[i * TQ + blk * HB:i * TQ + (blk + 1) * HB, :] = _rms_rows(o, sg_ref[...]).astype(BF16)


def _diff_attn_bounded(slopes, lam, bound, q, kaug, vaug, subg):
    smem = pl.BlockSpec(memory_space=pltpu.SMEM)
    bh = lambda b, h: (b, h)
    return pl.pallas_call(
        _diff_attn_bounded_kernel,
        grid=(BATCH, A_HEADS),
        in_specs=[smem, smem, smem,
                  pl.BlockSpec((SEQ, A_VDIM), bh), pl.BlockSpec((SEQ, 2 * A_VDIM), bh),
                  pl.BlockSpec((SEQ, 2 * A_VDIM), bh), pl.BlockSpec((1, A_VDIM), lambda b, h: (0, 0))],
        out_specs=pl.BlockSpec((SEQ, A_VDIM), bh),
        out_shape=jax.ShapeDtypeStruct((N_ROWS, A_WIDTH), BF16),
        compiler_params=_params(("arbitrary", "arbitrary")),
        name="diff_attn_bounded",
    )(slopes, lam, bound, q, kaug, vaug, subg)


def _ab_mix(a_ref, u_ref, x_ref, wg_ref, ps_ref, wo_ref, uext_ref, carry_ref):
    tm = ROW_TILE
    tiles_per_seq = SEQ // tm
    it = pl.program_id(0) % tiles_per_seq

    @pl.when(it == 0)
    def _():
        carry_ref[...] = jnp.zeros(carry_ref.shape, F32)

    u = u_ref[...]
    uext_ref[0:POOL_HALO, :] = carry_ref[...]
    uext_ref[POOL_HALO:, :] = u
    carry_ref[...] = u[tm - POOL_HALO:, :]
    pos = it * tm + lax.broadcasted_iota(jnp.int32, (tm, 1), 0)
    mixed = []
    for g, w in enumerate(POOL_WINDOWS):
        lo, hi = g * B_GROUP_DIM, (g + 1) * B_GROUP_DIM
        ug = u[:, lo:hi]
        win = ug
        for d in range(1, w):
            win = win + uext_ref[pl.ds(POOL_HALO - d, tm), lo:hi]
        cnt = jnp.minimum(pos + 1, w).astype(F32)
        pooled = win / cnt - ug
        mg = jnp.dot(pooled.astype(BF16), wg_ref[g], preferred_element_type=F32)
        mixed.append((mg * ps_ref[:, lo:hi]).astype(BF16))
    mix = jnp.concatenate([a_ref[...]] + mixed, axis=1)
    return x_ref[...] + jnp.dot(mix, wo_ref[...], preferred_element_type=F32)


def _c_in_kernel(x_ref, g_ref, w_ref, bd_ref, qg_ref, kg_ref, q_ref, k_ref, v_ref):
    xn = _rms_rows(x_ref[...], g_ref[...]).astype(BF16)
    bd = bd_ref[...]
    def finish(seg, cb):
        lo, hi = cb * NORM_SEG, (cb + 1) * NORM_SEG
        if cb < C_QW // NORM_SEG:
            q_ref[:, lo:hi] = _seg_norm(seg, bd, qg_ref[:, lo:hi]).astype(BF16)
        else:
            k_ref[...] = _seg_norm(seg, bd, kg_ref[...]).astype(BF16)

    pending = None
    for cb in range(C_QW // NORM_SEG + 1):
        seg = jnp.dot(xn, w_ref[:, cb * NORM_SEG:(cb + 1) * NORM_SEG], preferred_element_type=F32)
        if pending is not None:
            finish(*pending)
        pending = (seg, cb)
    v = jnp.dot(xn, w_ref[:, C_QW + NORM_SEG:], preferred_element_type=F32).astype(BF16)
    finish(*pending)
    ones = jnp.ones((v.shape[0], LANES), BF16)
    for g in range(C_KV_HEADS):
        v_ref[:, g * 2 * LANES:g * 2 * LANES + LANES] = v[:, g * LANES:(g + 1) * LANES]
        v_ref[:, g * 2 * LANES + LANES:(g + 1) * 2 * LANES] = ones


def _c_in(x, g, w, o, bd, qg, kg):
    tm = PROJ_ROW_TILE
    row = lambda i: (i, 0)
    kvw = 2 * C_KV_HEADS * C_HEAD_DIM
    return pl.pallas_call(
        _c_in_kernel,
        grid=(N_ROWS // tm,),
        in_specs=[pl.BlockSpec((tm, D_MODEL), row), _const_spec((1, D_MODEL)),
                  _const_spec((D_MODEL, C_QW + 2 * kvw), o), _const_spec((NORM_SEG, NORM_SEG)),
                  _const_spec((1, C_QW)), _const_spec((1, kvw))],
        out_specs=[pl.BlockSpec((tm, C_QW), row), pl.BlockSpec((tm, kvw), row), pl.BlockSpec((tm, 2 * kvw), row)],
        out_shape=[jax.ShapeDtypeStruct((N_ROWS, C_QW), BF16), jax.ShapeDtypeStruct((N_ROWS, kvw), BF16),
                   jax.ShapeDtypeStruct((N_ROWS, 2 * kvw), BF16)],
        compiler_params=_params(("arbitrary",)),
        name="c_in",
    )(x, g, w, bd, qg, kg)


def _swa_kernel(q_ref, kp_ref, kc_ref, vp_ref, vc_ref, bias_ref, sink_ref, o_ref):
    Q = C_QBLK
    i = pl.program_id(1)
    q = q_ref[...]
    lane = lax.broadcasted_iota(jnp.int32, (Q, LANES), 1)
    first_half = lane < C_HEAD_DIM
    zero = jnp.zeros((Q, LANES), BF16)
    rows = []
    for p in range(C_PAIRS):
        qp = q[:, p * LANES:(p + 1) * LANES]
        rows.append(jnp.where(first_half, qp, zero))
        rows.append(jnp.where(first_half, zero, qp))
    qs = jnp.concatenate(rows, axis=0)
    k2 = jnp.concatenate([kp_ref[...], kc_ref[...]], axis=0)
    v2 = jnp.concatenate([vp_ref[...], vc_ref[...]], axis=0)
    s = lax.dot_general(qs, k2, (((1,), (1,)), ((), ())), preferred_element_type=F32)
    s = s.reshape(C_GROUP, Q, 2 * Q) + bias_ref[...]
    col = lax.broadcasted_iota(jnp.int32, (C_GROUP, Q, 2 * Q), 2)
    s = jnp.where(jnp.logical_and(i == 0, col < Q), MASK_VALUE, s)
    sink = sink_ref[...][:, :, 0:1]
    m = jnp.maximum(jnp.max(s, axis=-1, keepdims=True), sink)
    p = jnp.exp(s - m)
    denom = jnp.sum(p, axis=-1, keepdims=True) + jnp.exp(sink - m)
    pn = (p / denom).reshape(C_GROUP * Q, 2 * Q).astype(BF16)
    o = jnp.dot(pn, v2, preferred_element_type=F32)
    for p_ in range(C_PAIRS):
        oa = o[(2 * p_) * Q:(2 * p_ + 1) * Q, :]
        ob = o[(2 * p_ + 1) * Q:(2 * p_ + 2) * Q, :]
        o_ref[:, p_ * LANES:(p_ + 1) * LANES] = jnp.where(first_half, oa, ob).astype(BF16)


def _swa_online(q, kdup, vaug, bias, sink):
    Q = C_QBLK
    nb = SEQ // Q
    gw = C_GROUP * C_HEAD_DIM
    cur = lambda b, i, g: (b * nb + i, g)
    prev = lambda b, i, g: (b * nb + jnp.maximum(i - 1, 0), g)
    vcur = lambda b, i, g: (b * nb + i, 2 * g)
    vprev = lambda b, i, g: (b * nb + jnp.maximum(i - 1, 0), 2 * g)
    return pl.pallas_call(
        _swa_kernel,
        grid=(BATCH, nb, C_KV_HEADS),
        in_specs=[pl.BlockSpec((Q, gw), cur),
                  pl.BlockSpec((Q, LANES), prev), pl.BlockSpec((Q, LANES), cur),
                  pl.BlockSpec((Q, LANES), vprev), pl.BlockSpec((Q, LANES), vcur),
                  pl.BlockSpec((C_GROUP, Q, 2 * Q), lambda b, i, g: (g, 0, 0)),
                  pl.BlockSpec((C_GROUP, 1, LANES), lambda b, i, g: (g, 0, 0))],
        out_specs=pl.BlockSpec((Q, gw), cur),
        out_shape=jax.ShapeDtypeStruct((N_ROWS, C_QW), BF16),
        compiler_params=_params(("arbitrary", "arbitrary", "arbitrary")),
        name="swa_online",
    )(q, kdup, kdup, vaug, vaug, bias, sink)


def _swa_bounded_kernel(sterm_ref, q_ref, kp_ref, kc_ref, vp_ref, vc_ref, tab_ref, o_ref):
    Q = C_QBLK
    first = jnp.where(pl.program_id(1) == 0, 1, 0)
    lane = lax.broadcasted_iota(jnp.int32, (Q, LANES), 1)
    first_half = lane < C_HEAD_DIM
    zero = jnp.zeros((Q, LANES), BF16)
    nt = (((1,), (1,)), ((), ()))
    for g in range(C_KV_HEADS):
        kcol = slice(g * LANES, (g + 1) * LANES)
        vcol = slice(g * 2 * LANES, (g + 1) * 2 * LANES)
        for sub in range(SWA_BLOCKS):
            rows = []
            for p in range(C_PAIRS):
                c0 = g * C_GROUP * C_HEAD_DIM + p * LANES
                qp = q_ref[sub * Q:(sub + 1) * Q, c0:c0 + LANES]
                rows.append(jnp.where(first_half, qp, zero))
                rows.append(jnp.where(first_half, zero, qp))
            qs = jnp.concatenate(rows, axis=0)
            if sub == 0:
                k2 = jnp.concatenate([kp_ref[:, kcol], kc_ref[0:Q, kcol]], axis=0)
                v3 = jnp.concatenate([vp_ref[:, vcol], vc_ref[0:Q, vcol]], axis=0)
                tab = tab_ref[first, g * C_GROUP:(g + 1) * C_GROUP]
            else:
                k2 = kc_ref[(sub - 1) * Q:(sub + 1) * Q, kcol]
                v3 = vc_ref[(sub - 1) * Q:(sub + 1) * Q, vcol]
                tab = tab_ref[0, g * C_GROUP:(g + 1) * C_GROUP]
            s = lax.dot_general(qs, k2, nt, preferred_element_type=F32)
            p_ = jnp.exp(s.reshape(C_GROUP, Q, 2 * Q) + tab).astype(BF16).reshape(C_GROUP * Q, 2 * Q)
            o = jnp.dot(p_, v3, preferred_element_type=F32)
            for pr in range(C_PAIRS):
                outs = []
                for hh in (2 * pr, 2 * pr + 1):
                    blk = o[hh * Q:(hh + 1) * Q, :]
                    outs.append(blk[:, 0:LANES] / (blk[:, LANES:] + sterm_ref[g * C_GROUP + hh]))
                c0 = g * C_GROUP * C_HEAD_DIM + pr * LANES
                o_ref[sub * Q:(sub + 1) * Q, c0:c0 + LANES] = jnp.where(first_half, outs[0], outs[1]).astype(BF16)


def _swa_bounded(sterm, q, kdup, vaug, tab):
    Q, nq = C_QBLK, SWA_BLOCKS
    steps = SEQ // (nq * Q)
    kvw = 2 * C_KV_HEADS * C_HEAD_DIM
    cur = lambda b, i: (b * steps + i, 0)
    prev = lambda b, i: (b * nq * steps + jnp.maximum(nq * i - 1, 0), 0)
    return pl.pallas_call(
        _swa_bounded_kernel,
        grid=(BATCH, steps),
        in_specs=[pl.BlockSpec(memory_space=pltpu.SMEM),
                  pl.BlockSpec((nq * Q, C_QW), cur),
                  pl.BlockSpec((Q, kvw), prev), pl.BlockSpec((nq * Q, kvw), cur),
                  pl.BlockSpec((Q, 2 * kvw), prev), pl.BlockSpec((nq * Q, 2 * kvw), cur),
                  _const_spec((2, C_Q_HEADS, Q, 2 * Q))],
        out_specs=pl.BlockSpec((nq * Q, C_QW), cur),
        out_shape=jax.ShapeDtypeStruct((N_ROWS, C_QW), BF16),
        compiler_params=_params(("arbitrary", "arbitrary")),
        name="swa_bounded",
    )(sterm, q, kdup, kdup, vaug, vaug, tab)


def _ffn_tail(x, g_ref, wup_ref, cw_ref, cb_ref, wd_ref, o_ref, hs_ref, carry_ref, act_ref):
    tm = ROW_TILE
    H = CONV_HALO

    @pl.when(pl.program_id(0) % (SEQ // tm) == 0)
    def _():
        carry_ref[...] = jnp.zeros(carry_ref.shape, F32)

    xn = _rms_rows(x, g_ref[...]).astype(BF16)

    def conv(c, part):
        cols = slice(part * D_FF + c * FF_CHUNK, part * D_FF + (c + 1) * FF_CHUNK)
        h = jnp.dot(xn, wup_ref[:, cols], preferred_element_type=F32)
        buf = hs_ref.at[c % 2, part]
        buf[0:H, :] = carry_ref[part, c]
        buf[H:, :] = h
        carry_ref[part, c] = h[tm - H:, :]
        return (cb_ref[:, cols] + cw_ref[2:3, cols] * h + cw_ref[1:2, cols] * buf[pl.ds(H - 1, tm), :]
                + cw_ref[0:1, cols] * buf[pl.ds(H - 2, tm), :])

    for c in range(FF_NCHUNK):
        gate = conv(c, 0)
        up = conv(c, 1)
        act = gate / (1.0 + jnp.exp(-gate)) * up
        act_ref[:, c * FF_CHUNK:(c + 1) * FF_CHUNK] = act.astype(BF16)
    o_ref[...] = x + jnp.dot(act_ref[...], wd_ref[...], preferred_element_type=F32)


def _ab_ffn_kernel(a_ref, u_ref, x_ref, wg_ref, ps_ref, wo_ref, g_ref, wup_ref, cw_ref, cb_ref, wd_ref, o_ref,
                   hs_ref, carry_ref, act_ref, uext_ref, ucarry_ref):
    x = _ab_mix(a_ref, u_ref, x_ref, wg_ref, ps_ref, wo_ref, uext_ref, ucarry_ref)
    _ffn_tail(x, g_ref, wup_ref, cw_ref, cb_ref, wd_ref, o_ref, hs_ref, carry_ref, act_ref)


def _c_ffn_kernel(att_ref, x_ref, wo_ref, g_ref, wup_ref, cw_ref, cb_ref, wd_ref, o_ref, hs_ref, carry_ref, act_ref):
    x = x_ref[...] + jnp.dot(att_ref[...], wo_ref[...], preferred_element_type=F32)
    _ffn_tail(x, g_ref, wup_ref, cw_ref, cb_ref, wd_ref, o_ref, hs_ref, carry_ref, act_ref)


def _ffn_specs(layer):
    tm = ROW_TILE
    in_specs = [_const_spec((1, D_MODEL), layer), _const_spec((D_MODEL, 2 * D_FF), layer),
                _const_spec((CONV_W, 2 * D_FF), layer), _const_spec((1, 2 * D_FF), layer),
                _const_spec((D_FF, D_MODEL), layer)]
    scratch = [pltpu.VMEM((2, 2, tm + CONV_HALO, FF_CHUNK), F32),
               pltpu.VMEM((2, FF_NCHUNK, CONV_HALO, FF_CHUNK), F32),
               pltpu.VMEM((tm, D_FF), BF16)]
    return in_specs, scratch


def _ab_ffn(a, u, x, wg, ps, wo, e, ffn_args, layer):
    tm = ROW_TILE
    row = lambda i: (i, 0)
    f_in, f_scratch = _ffn_specs(layer)
    return pl.pallas_call(
        _ab_ffn_kernel,
        grid=(N_ROWS // tm,),
        in_specs=[pl.BlockSpec((tm, A_WIDTH), row), pl.BlockSpec((tm, B_WIDTH), row),
                  pl.BlockSpec((tm, D_MODEL), row),
                  _const_spec((len(POOL_WINDOWS), B_GROUP_DIM, B_GROUP_DIM), e),
                  _const_spec((1, B_WIDTH)), _const_spec((A_WIDTH + B_WIDTH, D_MODEL), e)] + f_in,
        out_specs=pl.BlockSpec((tm, D_MODEL), row),
        out_shape=jax.ShapeDtypeStruct((N_ROWS, D_MODEL), F32),
        scratch_shapes=f_scratch + [pltpu.VMEM((tm + POOL_HALO, B_WIDTH), F32), pltpu.VMEM((POOL_HALO, B_WIDTH), F32)],
        compiler_params=_params(("arbitrary",)),
        name="ab_ffn",
    )(a, u, x, wg, ps, wo, *ffn_args)


def _c_ffn(att, x, wo, o, ffn_args, layer):
    tm = ROW_TILE
    row = lambda i: (i, 0)
    f_in, f_scratch = _ffn_specs(layer)
    return pl.pallas_call(
        _c_ffn_kernel,
        grid=(N_ROWS // tm,),
        in_specs=[pl.BlockSpec((tm, C_QW), row), pl.BlockSpec((tm, D_MODEL), row),
                  _const_spec((C_QW, D_MODEL), o)] + f_in,
        out_specs=pl.BlockSpec((tm, D_MODEL), row),
        out_shape=jax.ShapeDtypeStruct((N_ROWS, D_MODEL), F32),
        scratch_shapes=f_scratch,
        compiler_params=_params(("arbitrary",)),
        name="c_ffn",
    )(att, x, wo, *ffn_args)


def _alibi_slopes(n):
    return 2.0 ** (-8.0 * jnp.arange(1, n + 1, dtype=F32) / n)


def _block_diag_ones():
    seg = jnp.arange(NORM_SEG) // 64
    return (seg[:, None] == seg[None, :]).astype(BF16)


def _row(v):
    return v.reshape(1, -1).astype(F32)


def kernel(x, ab_norm, ab_w_in, a_q_norm, a_k_norm, a_lambda, a_sub_norm, b_w_group, b_scale, ab_w_out,
           c_norm, c_w_in, c_q_norm, c_k_norm, c_sinks, c_w_out, f_norm, f_w_up, f_conv, f_conv_b, f_w_down):
    bd = _block_diag_ones()
    h = x.reshape(N_ROWS, D_MODEL)

    a_slopes = _alibi_slopes(A_HEADS)
    ti = jnp.arange(ATT_TILE, dtype=F32)
    a_dtab = -a_slopes[:, None, None] * (ti[:, None] - ti[None, :])[None]
    c_slopes = _alibi_slopes(C_Q_HEADS)
    rel = C_QBLK + jnp.arange(C_QBLK)[:, None] - jnp.arange(2 * C_QBLK)[None, :]
    c_valid = (rel >= 0) & (rel < C_WINDOW)
    c_bias = jnp.where(c_valid[None], -c_slopes[:, None, None] * rel.astype(F32)[None], MASK_VALUE)

    ffn_args = (f_norm.astype(F32)[:, None, :], f_w_up.astype(BF16), f_conv.astype(F32),
                f_conv_b.astype(F32)[:, None, :], f_w_down.astype(BF16))
    ab_w_in_b, ab_w_out_b, b_w_group_b = ab_w_in.astype(BF16), ab_w_out.astype(BF16), b_w_group.astype(BF16)
    c_w_out_b = c_w_out.astype(BF16)
    hd = C_HEAD_DIM
    kcols = [C_QW + g * hd for g in range(C_KV_HEADS) for _ in range(2)]
    vcols = [c0 + C_KV_HEADS * hd for c0 in kcols]
    c_w_cat = jnp.concatenate([c_w_in[:, :, :C_QW]] + [c_w_in[:, :, c0:c0 + hd] for c0 in kcols + vcols],
                              axis=2).astype(BF16)

    for layer in range(DEPTH):
        if layer % 2 == 0:
            e = layer // 2
            lam_init = 0.8 - 0.6 * math.exp(-0.3 * layer)
            lp = a_lambda[e].astype(F32)
            lam = jnp.exp(jnp.sum(lp[0] * lp[1])) - jnp.exp(jnp.sum(lp[2] * lp[3])) + lam_init
            qg = _row(jnp.tile(a_q_norm[e], 2 * A_HEADS)) * (A_HEAD_DIM ** -0.5)
            kg = _row(jnp.tile(a_k_norm[e], 2 * A_HEADS))
            q, kaug, vaug, u = _ab_in(h, _row(ab_norm[e]), ab_w_in_b, e, bd, qg, kg)
            subg = _row(a_sub_norm[e]) * (1.0 - lam_init)
            lam1 = lam.reshape(1)
            bound = (1.02 * A_HEAD_DIM ** 0.5) * jnp.max(jnp.abs(a_q_norm[e])) * jnp.max(jnp.abs(a_k_norm[e]))
            bound = bound.astype(F32).reshape(1)
            a_out = lax.cond(
                bound[0] <= SCORE_BOUND_LIMIT,
                lambda: _diff_attn_bounded(a_slopes, lam1, bound, q, kaug, vaug, subg),
                lambda: _diff_attn_online(a_slopes, lam1, q, kaug, vaug, a_dtab, subg))
            h = _ab_ffn(a_out, u, h, b_w_group_b, _row(b_scale[e]), ab_w_out_b, e, ffn_args, layer)
        else:
            o = layer // 2
            qg = _row(jnp.tile(c_q_norm[o], C_Q_HEADS)) * (C_HEAD_DIM ** -0.5)
            kg = _row(jnp.tile(c_k_norm[o], 2 * C_KV_HEADS))
            q, kdup, vaug = _c_in(h, _row(c_norm[o]), c_w_cat, o, bd, qg, kg)
            sinks = c_sinks[o].astype(F32)
            bound = (1.02 * C_HEAD_DIM ** 0.5) * jnp.max(jnp.abs(c_q_norm[o])) * jnp.max(jnp.abs(c_k_norm[o]))
            bound = bound.astype(F32)
            shift = jnp.maximum(bound, sinks)
            tab = c_bias - shift[:, None, None]
            tab = jnp.stack([tab, jnp.where(jnp.arange(2 * C_QBLK) < C_QBLK, MASK_VALUE, tab)])
            sterm = jnp.exp(sinks - shift)
            sink_b = jnp.broadcast_to(sinks[:, None, None], (C_Q_HEADS, 1, LANES))
            att = lax.cond(
                bound <= SCORE_BOUND_LIMIT,
                lambda: _swa_bounded(sterm, q, kdup, vaug, tab),
                lambda: _swa_online(q, kdup, vaug, c_bias, sink_b))
            h = _c_ffn(att, h, c_w_out_b, o, ffn_args, layer)
    return h.reshape(BATCH, SEQ, D_MODEL)
```
